```python
import math
import jax
import jax.numpy as jnp
from jax import lax
import numpy as np

D_MODEL = 2048
BATCH = 4
SEQ = 8192
DEPTH = 4

ATTN_GROUPS = ((128, 1), (512, 4), (2048, 16))
ATTN_HEADS = 8
ATTN_HEAD_DIM = 128
ATTN_WIDTH = ATTN_HEADS * ATTN_HEAD_DIM
ALIBI_SLOPES = tuple(2.0 ** (-8.0 * (j + 1) / ATTN_HEADS) for j in range(ATTN_HEADS))
SSM_EXPAND = 1
SSM_INNER = SSM_EXPAND * D_MODEL
SSM_HEAD_DIM = 64
SSM_HEADS = SSM_INNER // SSM_HEAD_DIM
SSM_GROUPS = 8
SSM_HEADS_PER_GROUP = SSM_HEADS // SSM_GROUPS
SSM_STATE = 128
SSM_CONV = 5
SSM_CHUNK = 128
XBC_WIDTH = SSM_INNER + 2 * SSM_GROUPS * SSM_STATE
MLP_HIDDEN = 4 * D_MODEL
N_BRANCH = 2
NORM_EPS = 1e-6
QKV_COLS = 3 * len(ATTN_GROUPS) * ATTN_WIDTH
OFF_Z = QKV_COLS
OFF_XBC = OFF_Z + SSM_INNER
OFF_DT = OFF_XBC + XBC_WIDTH
OFF_GATE = OFF_DT + 2 * SSM_HEADS
IN_WIDTH = OFF_GATE + N_BRANCH * D_MODEL

kernel_name = "hybrid_dilated_attn_bissd_encoder"


def rms_norm(x, gain):
    xf = x.astype(jnp.float32)
    y = xf * lax.rsqrt(jnp.mean(jnp.square(xf), axis=-1, keepdims=True) + NORM_EPS)
    return (y * gain.astype(jnp.float32)).astype(x.dtype)


def dilated_window_attention(q, k, v, window, dilation):
    b, s, h, e = q.shape
    half = window // (2 * dilation)
    n = s // dilation
    nb = -(-n // half)
    pad_len = nb * half

    def to_sub(t):
        return t.reshape(b, n, dilation, h, e).transpose(0, 2, 1, 3, 4)

    def key_blocks(t):
        t = jnp.pad(to_sub(t), ((0, 0), (0, 0), (half, pad_len - n + half), (0, 0), (0, 0)))
        t = t.reshape(b, dilation, nb + 2, half, h, e)
        return jnp.concatenate([t[:, :, :-2], t[:, :, 1:-1], t[:, :, 2:]], axis=3)

    qs = jnp.pad(to_sub(q), ((0, 0), (0, 0), (0, pad_len - n), (0, 0), (0, 0)))
    qs = qs.reshape(b, dilation, nb, half, h, e)
    kb = key_blocks(k)
    vb = key_blocks(v)
    scores = jnp.einsum('bdnqhe,bdnkhe->bdhnqk', qs, kb,
                        preferred_element_type=jnp.float32) * (e ** -0.5)
    qi = jnp.arange(half)[:, None]
    kj = jnp.arange(3 * half)[None, :]
    rel = kj - half - qi
    key_pos = jnp.arange(nb)[:, None, None] * half - half + kj[None]
    valid = (jnp.abs(rel) <= half)[None] & (key_pos >= 0) & (key_pos < n)
    slopes = jnp.asarray(ALIBI_SLOPES, jnp.float32)
    dist = (jnp.abs(rel) * dilation).astype(jnp.float32)
    bias = -slopes[:, None, None] * dist[None]
    scores = jnp.where(valid, scores + bias[:, None], -jnp.inf)
    lse = jax.nn.logsumexp(scores, axis=-1, keepdims=True)
    p = jnp.exp(scores - lse)
    o = jnp.einsum('bdhnqk,bdnkhe->bdnqhe', p.astype(vb.dtype), vb)
    o = o.reshape(b, dilation, pad_len, h, e)[:, :, :n].transpose(0, 2, 1, 3, 4).reshape(b, s, h, e)
    lse = lse[..., 0].reshape(b, dilation, h, pad_len)[..., :n].transpose(0, 3, 1, 2).reshape(b, s, h)
    return o, lse


def dilated_attention_branch(h, w_qkv):
    b, s, _ = h.shape
    outs, lses = [], []
    for g_idx, (window, dilation) in enumerate(ATTN_GROUPS):
        cols = w_qkv[:, g_idx * 3 * ATTN_WIDTH:(g_idx + 1) * 3 * ATTN_WIDTH]
        qkv = (h @ cols).reshape(b, s, 3, ATTN_HEADS, ATTN_HEAD_DIM)
        o, lse = dilated_window_attention(qkv[:, :, 0], qkv[:, :, 1], qkv[:, :, 2], window, dilation)
        outs.append(o)
        lses.append(lse)
    wts = jax.nn.softmax(jnp.stack(lses, axis=0), axis=0)
    o = jnp.sum(wts[..., None] * jnp.stack(outs, axis=0).astype(jnp.float32), axis=0)
    return o.reshape(b, s, ATTN_WIDTH).astype(h.dtype)


def centred_depthwise_conv(x, w, bias):
    y = lax.conv_general_dilated(x, w.astype(x.dtype), window_strides=(1,),
                                 padding=[(SSM_CONV // 2, SSM_CONV // 2)],
                                 dimension_numbers=('NWC', 'WIO', 'NWC'),
                                 feature_group_count=x.shape[-1])
    return y + bias.astype(x.dtype)


def ssd_chunked_scan(xh, dt, a, bm, cm):
    b, s, g, r, p = xh.shape
    n = bm.shape[-1]
    l = SSM_CHUNK
    nc = s // l
    f32 = jnp.float32
    xdt = (xh.astype(f32) * dt[..., None]).reshape(b, nc, l, g, r, p)
    a_cs = jnp.cumsum((dt * a).reshape(b, nc, l, g, r), axis=2)
    bm = bm.astype(f32).reshape(b, nc, l, g, n)
    cm = cm.astype(f32).reshape(b, nc, l, g, n)
    lower = jnp.tril(jnp.ones((l, l), dtype=bool))[:, :, None, None]
    seg = a_cs[:, :, :, None] - a_cs[:, :, None, :]
    decay = jnp.exp(jnp.where(lower, seg, -jnp.inf))
    cb = jnp.einsum('bclgn,bcsgn->bclsg', cm, bm)
    y_diag = jnp.einsum('bclsgr,bcsgrp->bclgrp', cb[..., None] * decay, xdt)
    decay_to_end = jnp.exp(a_cs[:, :, -1:] - a_cs)
    states = jnp.einsum('bclgn,bclgrp->bcgrpn', bm, xdt * decay_to_end[..., None])
    chunk_decay = jnp.exp(a_cs[:, :, -1])

    def carry_state(hs, inp):
        st, dec = inp
        return hs * dec[..., None, None] + st, hs

    h0 = jnp.zeros((b, g, r, p, n), f32)
    _, h_in = lax.scan(carry_state, h0, (jnp.moveaxis(states, 1, 0), jnp.moveaxis(chunk_decay, 1, 0)))
    h_in = jnp.moveaxis(h_in, 0, 1)
    y_off = jnp.einsum('bclgn,bcgrpn->bclgrp', cm, h_in) * jnp.exp(a_cs)[..., None]
    return (y_diag + y_off).reshape(b, s, g, r, p)


def bidirectional_ssd(z, xbc, dt_raw_f, dt_raw_b, conv_w, conv_b, a_log, dt_bias, d_skip, norm_g):
    b, s, _ = xbc.shape
    f32 = jnp.float32
    G, R = SSM_GROUPS, SSM_HEADS_PER_GROUP
    xbc = jax.nn.silu(centred_depthwise_conv(xbc, conv_w, conv_b))
    xs, bm, cm = jnp.split(xbc, [SSM_INNER, SSM_INNER + SSM_GROUPS * SSM_STATE], axis=-1)
    xh = xs.reshape(b, s, G, R, SSM_HEAD_DIM)
    bm = bm.reshape(b, s, G, SSM_STATE)
    cm = cm.reshape(b, s, G, SSM_STATE)
    dt_f = jax.nn.softplus(dt_raw_f.astype(f32) + dt_bias[0].astype(f32)).reshape(b, s, G, R)
    dt_b = jax.nn.softplus(dt_raw_b.astype(f32) + dt_bias[1].astype(f32)).reshape(b, s, G, R)
    a_f = -jnp.exp(a_log[0].astype(f32)).reshape(G, R)
    a_b = -jnp.exp(a_log[1].astype(f32)).reshape(G, R)
    flip = lambda t: jnp.flip(t, axis=1)
    y_fwd = ssd_chunked_scan(xh, dt_f, a_f, bm, cm)
    y_bwd = flip(ssd_chunked_scan(flip(xh), flip(dt_b), a_b, flip(bm), flip(cm)))
    y = y_fwd + y_bwd + xh.astype(f32) * d_skip.astype(f32).reshape(G, R, 1)
    y = y.reshape(b, s, SSM_INNER) * jax.nn.silu(z.astype(f32))
    yg = y.reshape(b, s, G, SSM_INNER // G)
    yg = yg * lax.rsqrt(jnp.mean(jnp.square(yg), axis=-1, keepdims=True) + NORM_EPS)
    return (yg.reshape(b, s, SSM_INNER) * norm_g.astype(f32)).astype(z.dtype)


def hybrid_layer(x, c, w_mod, b_mod, norm_mix, w_in, b_gate, conv_w, conv_b, a_log, dt_bias,
                 d_skip, ssm_norm, w_attn_br, w_ssm_br, w_out, norm_mlp, w_mlp_in, w_mlp_out):
    mod = (c @ w_mod + b_mod)[:, None, :]
    sh_mix, sc_mix, gt_mix, sh_mlp, sc_mlp, gt_mlp = jnp.split(mod, 6, axis=-1)
    h = rms_norm(x, norm_mix) * (1.0 + sc_mix) + sh_mix
    y_attn = dilated_attention_branch(h, w_in[:, :QKV_COLS])
    z = h @ w_in[:, OFF_Z:OFF_XBC]
    xbc = h @ w_in[:, OFF_XBC:OFF_DT]
    dt_f = h @ w_in[:, OFF_DT:OFF_DT + SSM_HEADS]
    dt_b = h @ w_in[:, OFF_DT + SSM_HEADS:OFF_GATE]
    y_ssm = bidirectional_ssd(z, xbc, dt_f, dt_b, conv_w, conv_b, a_log, dt_bias, d_skip, ssm_norm)
    gates = jax.nn.sigmoid(h @ w_in[:, OFF_GATE:IN_WIDTH] + b_gate)
    g_attn, g_ssm = jnp.split(gates, N_BRANCH, axis=-1)
    merged = g_attn * (y_attn @ w_attn_br) + g_ssm * (y_ssm @ w_ssm_br)
    x = x + gt_mix * (merged @ w_out)
    h2 = rms_norm(x, norm_mlp) * (1.0 + sc_mlp) + sh_mlp
    u = jnp.square(jax.nn.relu(h2 @ w_mlp_in))
    return x + gt_mlp * (u @ w_mlp_out)


def setup_inputs(seed: int = 0) -> dict:
    key = jax.random.key(seed)
    ks = jax.random.split(key, 24)
    f32 = jnp.float32
    L, D = DEPTH, D_MODEL

    def normal(k, shape, std):
        return jax.random.normal(k, shape, f32) * std

    dt0 = jnp.exp(jax.random.uniform(ks[10], (L, 2, SSM_HEADS), f32, math.log(1e-3), math.log(1e-1)))
    return {
        "x": normal(ks[0], (BATCH, SEQ, D), 1.0),
        "c": normal(ks[1], (BATCH, D), 1.0),
        "w_mod": normal(ks[2], (L, D, 6 * D), 0.5 * D ** -0.5),
        "b_mod": normal(ks[3], (L, 6 * D), 0.02),
        "norm_mix": 1.0 + normal(ks[4], (L, D), 0.02),
        "w_in": normal(ks[5], (L, D, IN_WIDTH), D ** -0.5),
        "b_gate": normal(ks[6], (L, N_BRANCH * D), 0.02),
        "conv_w": normal(ks[7], (L, SSM_CONV, 1, XBC_WIDTH), SSM_CONV ** -0.5),
        "conv_b": normal(ks[8], (L, XBC_WIDTH), 0.02),
        "a_log": jnp.log(jax.random.uniform(ks[9], (L, 2, SSM_HEADS), f32, 1.0, 16.0)),
        "dt_bias": dt0 + jnp.log(-jnp.expm1(-dt0)),
        "d_skip": 1.0 + normal(ks[11], (L, SSM_HEADS), 0.02),
        "ssm_norm": 1.0 + normal(ks[12], (L, SSM_INNER), 0.02),
        "w_attn_br": normal(ks[13], (L, ATTN_WIDTH, D), ATTN_WIDTH ** -0.5),
        "w_ssm_br": normal(ks[14], (L, SSM_INNER, D), SSM_INNER ** -0.5),
        "w_out": normal(ks[15], (L, D, D), D ** -0.5),
        "norm_mlp": 1.0 + normal(ks[16], (L, D), 0.02),
        "w_mlp_in": normal(ks[17], (L, D, MLP_HIDDEN), D ** -0.5),
        "w_mlp_out": normal(ks[18], (L, MLP_HIDDEN, D), MLP_HIDDEN ** -0.5),
        "norm_final": 1.0 + normal(ks[19], (D,), 0.02),
    }


def reference(x, c, w_mod, b_mod, norm_mix, w_in, b_gate, conv_w, conv_b, a_log, dt_bias, d_skip,
              ssm_norm, w_attn_br, w_ssm_br, w_out, norm_mlp, w_mlp_in, w_mlp_out, norm_final):
    for i in range(DEPTH):
        x = hybrid_layer(x, c, w_mod[i], b_mod[i], norm_mix[i], w_in[i], b_gate[i], conv_w[i],
                         conv_b[i], a_log[i], dt_bias[i], d_skip[i], ssm_norm[i], w_attn_br[i],
                         w_ssm_br[i], w_out[i], norm_mlp[i], w_mlp_in[i], w_mlp_out[i])
    return rms_norm(x, norm_final)
```

```python
import functools
import math

import jax
import jax.numpy as jnp
from jax import lax
from jax.experimental import pallas as pl
from jax.experimental.pallas import tpu as pltpu

F32 = jnp.float32
BF16 = jnp.bfloat16

D_MODEL = 2048
ATTN_GROUPS = ((128, 1), (512, 4), (2048, 16))
ATTN_HEADS = 8
ATTN_HEAD_DIM = 128
ATTN_WIDTH = ATTN_HEADS * ATTN_HEAD_DIM
ATTN_HALF = 64
ALIBI_SLOPES = tuple(2.0 ** (-8.0 * (j + 1) / ATTN_HEADS) for j in range(ATTN_HEADS))
SSM_INNER = D_MODEL
SSM_HEAD_DIM = 64
SSM_HEADS = SSM_INNER // SSM_HEAD_DIM
SSM_GROUPS = 8
SSM_HEADS_PER_GROUP = SSM_HEADS // SSM_GROUPS
SSM_STATE = 128
SSM_CONV = 5
XBC_WIDTH = SSM_INNER + 2 * SSM_GROUPS * SSM_STATE
GROUP_WIDTH = SSM_HEADS_PER_GROUP * SSM_HEAD_DIM
MLP_HIDDEN = 4 * D_MODEL
NORM_EPS = 1e-6
QKV_GROUP_COLS = 3 * ATTN_WIDTH
QKV_COLS = len(ATTN_GROUPS) * QKV_GROUP_COLS
OFF_Z = QKV_COLS
OFF_XBC = OFF_Z + SSM_INNER
OFF_DT = OFF_XBC + XBC_WIDTH
OFF_GATE = OFF_DT + 2 * SSM_HEADS
IN_WIDTH = OFF_GATE + 2 * D_MODEL

LANES = 128
SSD_CHUNK = 128
CONV_HALO = 16
NEG_BIG = -1e30
VMEM_LIMIT = 56 * 1024 * 1024


def _params(sem, vmem=VMEM_LIMIT):
    return pltpu.CompilerParams(dimension_semantics=sem, vmem_limit_bytes=vmem)


def _dot(a, b):
    return jnp.dot(a, b, preferred_element_type=F32)


def _dot_nt(a, b):
    return lax.dot_general(a, b, (((1,), (1,)), ((), ())), preferred_element_type=F32)


def _dot_exact(a, b):
    return jnp.dot(a, b, preferred_element_type=F32, precision=lax.Precision.HIGHEST)


def _mod_kernel(c_ref, w_ref, b_ref, o_ref):
    o_ref[...] = _dot_exact(c_ref[...], w_ref[...]) + b_ref[...]


def _modulation(c, w_mod, b_mod):
    depth, d, n = w_mod.shape
    bsz = c.shape[0]
    tn = 1024
    return pl.pallas_call(
        _mod_kernel,
        out_shape=jax.ShapeDtypeStruct((depth, bsz, n), F32),
        grid=(depth, n // tn),
        in_specs=[
            pl.BlockSpec((bsz, d), lambda l, j: (0, 0)),
            pl.BlockSpec((None, d, tn), lambda l, j: (l, 0, j)),
            pl.BlockSpec((None, 1, tn), lambda l, j: (l, 0, j)),
        ],
        out_specs=pl.BlockSpec((None, bsz, tn), lambda l, j: (l, 0, j)),
        compiler_params=_params(("parallel", "parallel")),
        name="adaln_mod",
    )(c, w_mod, b_mod.reshape(depth, 1, n))


_IN_TN = 1024
_IN_SEGMENTS = ((0, 3), (3, 3), (6, 3), (9, 2), (11, 4), (15, 4))
_IN_TILES = 19
_NORM_ROWS = 256


def _modulated_norm(x_ref, gain_ref, shift, scale, r0, rows):
    xf = x_ref[pl.ds(r0, rows), :]
    ms = jnp.mean(xf * xf, axis=-1, keepdims=True)
    y = xf * lax.rsqrt(ms + NORM_EPS) * gain_ref[...]
    return y * (1.0 + scale) + shift


def _inproj_kernel(x_ref, mod_ref, gain_ref, w_ref, wdt_ref,
                   q0_ref, q1_ref, q2_ref, z_ref, xbc_ref, gate_ref, dt_ref, h_scr):
    j = pl.program_id(1)
    tm = x_ref.shape[0]

    @pl.when(j == 0)
    def _():
        shift = mod_ref[0:1, :]
        scale = mod_ref[1:2, :]
        for r in range(tm // _NORM_ROWS):
            r0 = r * _NORM_ROWS
            h = _modulated_norm(x_ref, gain_ref, shift, scale, r0, _NORM_ROWS)
            h_scr[pl.ds(r0, _NORM_ROWS), :] = h.astype(BF16)
            dt_ref[pl.ds(r0, _NORM_ROWS), :] = _dot_exact(h, wdt_ref[...])

    res = _dot(h_scr[...], w_ref[...]).astype(BF16)
    for (lo, cnt), ref in zip(_IN_SEGMENTS, (q0_ref, q1_ref, q2_ref, z_ref, xbc_ref, gate_ref)):
        @pl.when((j >= lo) & (j < lo + cnt))
        def _(ref=ref):
            ref[...] = res


def _in_projection(x2, mod_l, gain, w_main, w_dt, bsz):
    t, d = x2.shape
    seq = t // bsz
    tm = min(512, seq)
    tiles_per_batch = seq // tm
    tn = _IN_TN

    def seg_spec(lo, cnt):
        return pl.BlockSpec((tm, tn), lambda i, j: (i, jnp.clip(j - lo, 0, cnt - 1)))

    out_shapes = [jax.ShapeDtypeStruct((t, cnt * tn), BF16) for _, cnt in _IN_SEGMENTS]
    out_shapes.append(jax.ShapeDtypeStruct((t, LANES), F32))
    out_specs = [seg_spec(lo, cnt) for lo, cnt in _IN_SEGMENTS]
    out_specs.append(pl.BlockSpec((tm, LANES), lambda i, j: (i, 0)))
    return pl.pallas_call(
        _inproj_kernel,
        out_shape=out_shapes,
        grid=(t // tm, _IN_TILES),
        in_specs=[
            pl.BlockSpec((tm, d), lambda i, j: (i, 0)),
            pl.BlockSpec((None, 6, d), lambda i, j: (i // tiles_per_batch, 0, 0)),
            pl.BlockSpec((1, d), lambda i, j: (0, 0)),
            pl.BlockSpec((d, tn), lambda i, j: (0, j)),
            pl.BlockSpec((d, LANES), lambda i, j: (0, 0)),
        ],
        out_specs=out_specs,
        scratch_shapes=[pltpu.VMEM((tm, d), BF16)],
        compiler_params=_params(("parallel", "arbitrary")),
        name="in_projection",
    )(x2, mod_l, gain, w_main, w_dt)


def _attn_kernel(q_ref, kc_ref, vc_ref, kp_ref, vp_ref, kn_ref, vn_ref, o_ref, lse_ref,
                 *, dil, qb, nblk):
    j = pl.program_id(2)
    half = ATTN_HALF
    scale = ATTN_HEAD_DIM ** -0.5
    row = lax.broadcasted_iota(jnp.int32, (qb, qb), 0)
    col = lax.broadcasted_iota(jnp.int32, (qb, qb), 1)
    rel_c = jnp.abs(col - row)
    ok_c = rel_c <= half
    nd_c = -(rel_c * dil).astype(F32)
    rowh = lax.broadcasted_iota(jnp.int32, (qb, half), 0)
    colh = lax.broadcasted_iota(jnp.int32, (qb, half), 1)
    rel_p = rowh + half - colh
    ok_p = (rel_p <= half) & (j > 0)
    nd_p = -(rel_p * dil).astype(F32)
    rel_n = colh + qb - rowh
    ok_n = (rel_n <= half) & (j < nblk - 1)
    nd_n = -(rel_n * dil).astype(F32)
    lane = lax.broadcasted_iota(jnp.int32, (qb, LANES), 1)
    lse_all = jnp.zeros((qb, LANES), F32)
    for h in range(ATTN_HEADS):
        sl = slice(h * ATTN_HEAD_DIM, (h + 1) * ATTN_HEAD_DIM)
        slope = ALIBI_SLOPES[h]
        q = q_ref[:, sl]
        t_c = jnp.where(ok_c, _dot_nt(q, kc_ref[:, sl]) * scale + slope * nd_c, NEG_BIG)
        t_p = jnp.where(ok_p, _dot_nt(q, kp_ref[:, sl]) * scale + slope * nd_p, NEG_BIG)
        t_n = jnp.where(ok_n, _dot_nt(q, kn_ref[:, sl]) * scale + slope * nd_n, NEG_BIG)
        m = jnp.maximum(jnp.max(t_c, axis=-1, keepdims=True),
                        jnp.maximum(jnp.max(t_p, axis=-1, keepdims=True),
                                    jnp.max(t_n, axis=-1, keepdims=True)))
        p_c = jnp.exp(t_c - m)
        p_p = jnp.exp(t_p - m)
        p_n = jnp.exp(t_n - m)
        l = (jnp.sum(p_c, axis=-1, keepdims=True) + jnp.sum(p_p, axis=-1, keepdims=True)
             + jnp.sum(p_n, axis=-1, keepdims=True))
        acc = (_dot(p_c.astype(BF16), vc_ref[:, sl]) + _dot(p_p.astype(BF16), vp_ref[:, sl])
               + _dot(p_n.astype(BF16), vn_ref[:, sl]))
        o_ref[:, sl] = (acc * (1.0 / l)).astype(BF16)
        lse_all = jnp.where(lane == h, m + jnp.log(l), lse_all)
    lse_ref[...] = lse_all


def _attention_group(qkv, bsz, seq, dil):
    n = seq // dil
    qb = min(128, n)
    nblk = n // qb
    half = ATTN_HALF
    hb = qb // half
    nhalf = n // half
    w = ATTN_WIDTH
    view = qkv.reshape(bsz, n, dil * QKV_GROUP_COLS)
    cur = lambda part: pl.BlockSpec((None, qb, w), lambda b, r, j: (b, j, 3 * r + part))
    prev = lambda part: pl.BlockSpec(
        (None, half, w), lambda b, r, j: (b, jnp.maximum(j * hb - 1, 0), 3 * r + part))
    nxt = lambda part: pl.BlockSpec(
        (None, half, w), lambda b, r, j: (b, jnp.minimum((j + 1) * hb, nhalf - 1), 3 * r + part))
    o, lse = pl.pallas_call(
        functools.partial(_attn_kernel, dil=dil, qb=qb, nblk=nblk),
        out_shape=[jax.ShapeDtypeStruct((bsz, n, dil * w), BF16),
                   jax.ShapeDtypeStruct((bsz, n, dil * LANES), F32)],
        grid=(bsz, dil, nblk),
        in_specs=[cur(0), cur(1), cur(2), prev(1), prev(2), nxt(1), nxt(2)],
        out_specs=[pl.BlockSpec((None, qb, w), lambda b, r, j: (b, j, r)),
                   pl.BlockSpec((None, qb, LANES), lambda b, r, j: (b, j, r))],
        compiler_params=_params(("parallel", "parallel", "parallel")),
        name=f"band_attention_d{dil}",
    )(view, view, view, view, view, view, view)
    return o.reshape(bsz * seq, w), lse.reshape(bsz * seq, LANES)


def _softplus(x):
    return jnp.maximum(x, 0.0) + jnp.log(1.0 + jnp.exp(-jnp.abs(x)))


def _head_expand(cols, lane):
    out = cols[-1]
    for r in range(SSM_HEADS_PER_GROUP - 2, -1, -1):
        out = jnp.where(lane < (r + 1) * SSM_HEAD_DIM, cols[r], out)
    return out


def _dt_terms(dt_ref, par_ref):
    a_row = -jnp.exp(par_ref[0:1, :])
    dtv = _softplus(dt_ref[...] + par_ref[1:2, :])
    return dtv, dtv * a_row


def _ssd_fwd_kernel(xm_ref, xp_ref, xn_ref, dt_ref, cw_ref, cb_ref, par_ref, dsk_ref,
                    xc_ref, y1_ref, xf_scr, st_scr, *, nc):
    c = pl.program_id(1)
    L = SSD_CHUNK
    H = SSM_HEADS
    R = SSM_HEADS_PER_GROUP
    N = SSM_STATE
    GW = GROUP_WIDTH

    @pl.when(c == 0)
    def _():
        st_scr[...] = jnp.zeros_like(st_scr)

    has_prev = c > 0
    has_next = c < nc - 1
    cwid = 512
    for cc in range(XBC_WIDTH // cwid):
        cols = slice(cc * cwid, (cc + 1) * cwid)
        xm = xm_ref[:, cols].astype(F32)
        xp = jnp.where(has_prev, xp_ref[:, cols].astype(F32), 0.0)
        xn = jnp.where(has_next, xn_ref[:, cols].astype(F32), 0.0)
        xcat = jnp.concatenate([xp, xm, xn], axis=0)
        acc = cb_ref[:, cols] + cw_ref[0:1, cols] * xcat[CONV_HALO - 2:CONV_HALO - 2 + L]
        for k in range(1, SSM_CONV):
            off = CONV_HALO - 2 + k
            acc = acc + cw_ref[k:k + 1, cols] * xcat[off:off + L]
        sv = acc * (1.0 / (1.0 + jnp.exp(-acc)))
        xf_scr[:, cols] = sv
        xc_ref[:, cols] = sv.astype(BF16)

    dtv, dta = _dt_terms(dt_ref, par_ref)
    ti = lax.broadcasted_iota(jnp.int32, (L, L), 0)
    si = lax.broadcasted_iota(jnp.int32, (L, L), 1)
    lower = si <= ti
    strict_lower = si < ti
    strict_upper = si > ti
    a_f = _dot_exact(lower.astype(F32), dta)
    g_b = _dot_exact((si >= ti).astype(F32), dta)
    a_f_t = a_f.T
    g_b_t = g_b.T
    dt_t = dtv.T
    exp_af = jnp.exp(a_f)
    a_end = a_f[L - 1:L, :]
    w_end = jnp.exp(a_end - a_f) * dtv
    cd_row = jnp.exp(a_end)
    lane = lax.broadcasted_iota(jnp.int32, (L, GW), 1)
    lane_n = lax.broadcasted_iota(jnp.int32, (N, GW), 1)

    for g in range(SSM_GROUPS):
        gsl = slice(g * GW, (g + 1) * GW)
        xg = xf_scr[:, gsl]
        xg16 = xg.astype(BF16)
        bg = xf_scr[:, SSM_INNER + g * N:SSM_INNER + (g + 1) * N]
        cg16 = xf_scr[:, SSM_INNER + SSM_GROUPS * N + g * N:
                      SSM_INNER + SSM_GROUPS * N + (g + 1) * N].astype(BF16)
        bg16 = bg.astype(BF16)
        cb = _dot_nt(cg16, bg16)
        ms, xs = [], []
        for r in range(R):
            h = g * R + r
            e = jnp.where(lower, a_f[:, h:h + 1] - a_f_t[h:h + 1, :],
                          g_b[:, H + h:H + h + 1] - g_b_t[H + h:H + h + 1, :])
            dtf_row = dt_t[h:h + 1, :]
            dtb_row = dt_t[H + h:H + h + 1, :]
            dsel = jnp.where(strict_lower, dtf_row, jnp.where(strict_upper, dtb_row, dtf_row + dtb_row))
            ms.append((cb * (jnp.exp(e) * dsel)).astype(BF16))
            xs.append(jnp.where((lane >= r * SSM_HEAD_DIM) & (lane < (r + 1) * SSM_HEAD_DIM),
                                xg16, jnp.zeros_like(xg16)))
        y = _dot(jnp.concatenate(ms, axis=1), jnp.concatenate(xs, axis=0))
        st = st_scr[g]
        y = y + _dot(cg16, st.astype(BF16)) * _head_expand(
            [exp_af[:, g * R + r:g * R + r + 1] for r in range(R)], lane)
        y = y + xg * dsk_ref[:, gsl]
        y1_ref[:, gsl] = y.astype(BF16)
        xw = (xg * _head_expand([w_end[:, g * R + r:g * R + r + 1] for r in range(R)], lane)).astype(BF16)
        s_new = _dot(bg.T.astype(BF16), xw)
        st_scr[g] = st * _head_expand([cd_row[:, g * R + r:g * R + r + 1] for r in range(R)], lane_n) + s_new


def _ssd_bwd_kernel(xc_ref, dt_ref, par_ref, y1_ref, z_ref, ng_ref, y_ref, st_scr):
    c = pl.program_id(1)
    L = SSD_CHUNK
    H = SSM_HEADS
    R = SSM_HEADS_PER_GROUP
    N = SSM_STATE
    GW = GROUP_WIDTH

    @pl.when(c == 0)
    def _():
        st_scr[...] = jnp.zeros_like(st_scr)

    dtv, dta = _dt_terms(dt_ref, par_ref)
    ti = lax.broadcasted_iota(jnp.int32, (L, L), 0)
    si = lax.broadcasted_iota(jnp.int32, (L, L), 1)
    g_b = _dot_exact((si >= ti).astype(F32), dta)
    exp_gb = jnp.exp(g_b)
    g_start = g_b[0:1, :]
    w_start = jnp.exp(g_start - g_b) * dtv
    cd_row = jnp.exp(g_start)
    lane = lax.broadcasted_iota(jnp.int32, (L, GW), 1)
    lane_n = lax.broadcasted_iota(jnp.int32, (N, GW), 1)

    for g in range(SSM_GROUPS):
        gsl = slice(g * GW, (g + 1) * GW)
        xg = xc_ref[:, gsl].astype(F32)
        bg = xc_ref[:, SSM_INNER + g * N:SSM_INNER + (g + 1) * N].astype(F32)
        cg16 = xc_ref[:, SSM_INNER + SSM_GROUPS * N + g * N:SSM_INNER + SSM_GROUPS * N + (g + 1) * N]
        st = st_scr[g]
        heads = [H + g * R + r for r in range(R)]
        y = y1_ref[:, gsl].astype(F32) + _dot(cg16, st.astype(BF16)) * _head_expand(
            [exp_gb[:, h:h + 1] for h in heads], lane)
        zg = z_ref[:, gsl].astype(F32)
        y = y * (zg * (1.0 / (1.0 + jnp.exp(-zg))))
        ms = jnp.mean(y * y, axis=-1, keepdims=True)
        y_ref[:, gsl] = (y * lax.rsqrt(ms + NORM_EPS) * ng_ref[:, gsl]).astype(BF16)
        xw = (xg * _head_expand([w_start[:, h:h + 1] for h in heads], lane)).astype(BF16)
        s_new = _dot(bg.T.astype(BF16), xw)
        st_scr[g] = st * _head_expand([cd_row[:, h:h + 1] for h in heads], lane_n) + s_new


def _bidirectional_ssd(xbc, z, dt, conv_w, conv_b, ssm_par, d_skip_x, norm_g, bsz, seq):
    L = SSD_CHUNK
    nc = seq // L
    hpc = L // CONV_HALO
    nhalo = seq // CONV_HALO
    xbc3 = xbc.reshape(bsz, seq, XBC_WIDTH)
    dt3 = dt.reshape(bsz, seq, LANES)
    state = pltpu.VMEM((SSM_GROUPS, SSM_STATE, GROUP_WIDTH), F32)
    full = lambda shape: pl.BlockSpec(shape, lambda b, c: (0,) * len(shape))
    xc, y1 = pl.pallas_call(
        functools.partial(_ssd_fwd_kernel, nc=nc),
        out_shape=[jax.ShapeDtypeStruct((bsz, seq, XBC_WIDTH), BF16),
                   jax.ShapeDtypeStruct((bsz, seq, SSM_INNER), BF16)],
        grid=(bsz, nc),
        in_specs=[
            pl.BlockSpec((None, L, XBC_WIDTH), lambda b, c: (b, c, 0)),
            pl.BlockSpec((None, CONV_HALO, XBC_WIDTH), lambda b, c: (b, jnp.maximum(c * hpc - 1, 0), 0)),
            pl.BlockSpec((None, CONV_HALO, XBC_WIDTH),
                         lambda b, c: (b, jnp.minimum((c + 1) * hpc, nhalo - 1), 0)),
            pl.BlockSpec((None, L, LANES), lambda b, c: (b, c, 0)),
            full((SSM_CONV, XBC_WIDTH)), full((1, XBC_WIDTH)), full((8, LANES)), full((1, SSM_INNER)),
        ],
        out_specs=[pl.BlockSpec((None, L, XBC_WIDTH), lambda b, c: (b, c, 0)),
                   pl.BlockSpec((None, L, SSM_INNER), lambda b, c: (b, c, 0))],
        scratch_shapes=[pltpu.VMEM((L, XBC_WIDTH), F32), state],
        compiler_params=_params(("parallel", "arbitrary")),
        name="ssd_forward_sweep",
    )(xbc3, xbc3, xbc3, dt3, conv_w, conv_b, ssm_par, d_skip_x)
    rev = lambda width: pl.BlockSpec((None, L, width), lambda b, c: (b, nc - 1 - c, 0))
    y = pl.pallas_call(
        _ssd_bwd_kernel,
        out_shape=jax.ShapeDtypeStruct((bsz, seq, SSM_INNER), BF16),
        grid=(bsz, nc),
        in_specs=[rev(XBC_WIDTH), rev(LANES), full((8, LANES)), rev(SSM_INNER), rev(SSM_INNER),
                  full((1, SSM_INNER))],
        out_specs=rev(SSM_INNER),
        scratch_shapes=[state],
        compiler_params=_params(("parallel", "arbitrary")),
        name="ssd_backward_sweep",
    )(xc, dt3, ssm_par, y1, z.reshape(bsz, seq, SSM_INNER), norm_g)
    return y.reshape(bsz * seq, SSM_INNER)


def _merge_kernel(o0_ref, o1_ref, o2_ref, l0_ref, l1_ref, l2_ref, ys_ref, gate_ref, bg_ref,
                  wa_ref, ws_ref, out_ref):
    l0, l1, l2 = l0_ref[...], l1_ref[...], l2_ref[...]
    m = jnp.maximum(l0, jnp.maximum(l1, l2))
    e0, e1, e2 = jnp.exp(l0 - m), jnp.exp(l1 - m), jnp.exp(l2 - m)
    inv = 1.0 / (e0 + e1 + e2)
    w0, w1, w2 = e0 * inv, e1 * inv, e2 * inv
    parts = []
    for h in range(ATTN_HEADS):
        sl = slice(h * ATTN_HEAD_DIM, (h + 1) * ATTN_HEAD_DIM)
        parts.append((w0[:, h:h + 1] * o0_ref[:, sl].astype(F32)
                      + w1[:, h:h + 1] * o1_ref[:, sl].astype(F32)
                      + w2[:, h:h + 1] * o2_ref[:, sl].astype(F32)).astype(BF16))
    y_attn = jnp.concatenate(parts, axis=1)
    d = out_ref.shape[1]
    ga = gate_ref[:, :d].astype(F32) + bg_ref[:, :d]
    gs = gate_ref[:, d:].astype(F32) + bg_ref[:, d:]
    a = _dot(y_attn, wa_ref[...])
    s = _dot(ys_ref[...], ws_ref[...])
    out_ref[...] = (a * (1.0 / (1.0 + jnp.exp(-ga))) + s * (1.0 / (1.0 + jnp.exp(-gs)))).astype(BF16)


def _branch_merge(os_, lses, y_ssm, gate, b_gate, w_attn_br, w_ssm_br):
    t = y_ssm.shape[0]
    d = D_MODEL
    tm = 256
    row = lambda width: pl.BlockSpec((tm, width), lambda i: (i, 0))
    full = lambda shape: pl.BlockSpec(shape, lambda i: (0, 0))
    return pl.pallas_call(
        _merge_kernel,
        out_shape=jax.ShapeDtypeStruct((t, d), BF16),
        grid=(t // tm,),
        in_specs=[row(ATTN_WIDTH)] * 3 + [row(LANES)] * 3 + [row(SSM_INNER), row(2 * d), full((1, 2 * d)),
                                                               full((ATTN_WIDTH, d)), full((SSM_INNER, d))],
        out_specs=row(d),
        compiler_params=_params(("parallel",)),
        name="branch_merge",
    )(*os_, *lses, y_ssm, gate, b_gate, w_attn_br, w_ssm_br)


def _outproj_kernel(mg_ref, w_ref, x_ref, mod_ref, gain_ref, xo_ref, h2_ref):
    xn = x_ref[...] + mod_ref[2:3, :] * _dot(mg_ref[...], w_ref[...])
    xo_ref[...] = xn
    ms = jnp.mean(xn * xn, axis=-1, keepdims=True)
    y = xn * lax.rsqrt(ms + NORM_EPS) * gain_ref[...]
    h2_ref[...] = (y * (1.0 + mod_ref[4:5, :]) + mod_ref[3:4, :]).astype(BF16)


def _out_projection(merged, w_out, x2, mod_l, gain, bsz):
    t, d = x2.shape
    seq = t // bsz
    tm = 256
    tiles_per_batch = seq // tm
    row = lambda: pl.BlockSpec((tm, d), lambda i: (i, 0))
    return pl.pallas_call(
        _outproj_kernel,
        out_shape=[jax.ShapeDtypeStruct((t, d), F32), jax.ShapeDtypeStruct((t, d), BF16)],
        grid=(t // tm,),
        in_specs=[row(), pl.BlockSpec((d, d), lambda i: (0, 0)), row(),
                  pl.BlockSpec((None, 6, d), lambda i: (i // tiles_per_batch, 0, 0)),
                  pl.BlockSpec((1, d), lambda i: (0, 0))],
        out_specs=[row(), row()],
        compiler_params=_params(("parallel",)),
        name="out_projection",
    )(merged, w_out, x2, mod_l, gain)


def _mlp_kernel(h_ref, w1_ref, w2_ref, x_ref, mod_ref, fg_ref, o_ref, *, nk, final_norm):
    k = pl.program_id(1)
    u = jnp.maximum(_dot(h_ref[...], w1_ref[...]), 0.0)
    part = _dot((u * u).astype(BF16), w2_ref[...])

    @pl.when(k == 0)
    def _():
        o_ref[...] = part

    @pl.when(k > 0)
    def _():
        o_ref[...] += part

    @pl.when(k == nk - 1)
    def _():
        xn = x_ref[...] + mod_ref[5:6, :] * o_ref[...]
        if final_norm:
            ms = jnp.mean(xn * xn, axis=-1, keepdims=True)
            xn = xn * lax.rsqrt(ms + NORM_EPS) * fg_ref[...]
        o_ref[...] = xn


def _mlp(h2, w1, w2, x2, mod_l, final_gain, bsz, final_norm):
    t, d = x2.shape
    seq = t // bsz
    tm = 512
    tk = 512
    hidden = w1.shape[1]
    nk = hidden // tk
    tiles_per_batch = seq // tm
    return pl.pallas_call(
        functools.partial(_mlp_kernel, nk=nk, final_norm=final_norm),
        out_shape=jax.ShapeDtypeStruct((t, d), F32),
        grid=(t // tm, nk),
        in_specs=[pl.BlockSpec((tm, d), lambda i, k: (i, 0)),
                  pl.BlockSpec((d, tk), lambda i, k: (0, k)),
                  pl.BlockSpec((tk, d), lambda i, k: (k, 0)),
                  pl.BlockSpec((tm, d), lambda i, k: (i, 0)),
                  pl.BlockSpec((None, 6, d), lambda i, k: (i // tiles_per_batch, 0, 0)),
                  pl.BlockSpec((1, d), lambda i, k: (0, 0))],
        out_specs=pl.BlockSpec((tm, d), lambda i, k: (i, 0)),
        compiler_params=_params(("parallel", "arbitrary")),
        name="relu2_mlp",
    )(h2, w1, w2, x2, mod_l, final_gain)


def _pad_lanes(a):
    return jnp.pad(a, ((0, 0), (0, LANES - a.shape[1])))


def _layer(x2, bsz, seq, mod_l, norm_mix, w_in, b_gate, conv_w, conv_b, a_log, dt_bias, d_skip, ssm_norm,
           w_attn_br, w_ssm_br, w_out, norm_mlp, w_mlp_in, w_mlp_out, final_gain, final_norm):
    d = D_MODEL
    w_main = jnp.concatenate([w_in[:, :OFF_DT], w_in[:, OFF_GATE:]], axis=1).astype(BF16)
    w_dt = _pad_lanes(w_in[:, OFF_DT:OFF_GATE])
    ssm_par = jnp.pad(jnp.stack([a_log.reshape(-1), dt_bias.reshape(-1)]),
                      ((0, 6), (0, LANES - 2 * SSM_HEADS)))
    d_skip_x = jnp.repeat(d_skip, SSM_HEAD_DIM).reshape(1, SSM_INNER)

    q0, q1, q2, z, xbc, gate, dt = _in_projection(x2, mod_l, norm_mix.reshape(1, d), w_main, w_dt, bsz)
    os_, lses = [], []
    for qkv, (_, dil) in zip((q0, q1, q2), ATTN_GROUPS):
        o, lse = _attention_group(qkv, bsz, seq, dil)
        os_.append(o)
        lses.append(lse)
    y_ssm = _bidirectional_ssd(xbc, z, dt, conv_w.reshape(SSM_CONV, XBC_WIDTH), conv_b.reshape(1, XBC_WIDTH),
                               ssm_par, d_skip_x, ssm_norm.reshape(1, SSM_INNER), bsz, seq)
    merged = _branch_merge(os_, lses, y_ssm, gate, b_gate.reshape(1, 2 * d),
                           w_attn_br.astype(BF16), w_ssm_br.astype(BF16))
    x2, h2 = _out_projection(merged, w_out.astype(BF16), x2, mod_l, norm_mlp.reshape(1, d), bsz)
    return _mlp(h2, w_mlp_in.astype(BF16), w_mlp_out.astype(BF16), x2, mod_l, final_gain, bsz, final_norm)


def kernel(x, c, w_mod, b_mod, norm_mix, w_in, b_gate, conv_w, conv_b, a_log, dt_bias, d_skip, ssm_norm,
           w_attn_br, w_ssm_br, w_out, norm_mlp, w_mlp_in, w_mlp_out, norm_final):
    bsz, seq, d = x.shape
    depth = w_mod.shape[0]
    mod = _modulation(c, w_mod, b_mod).reshape(depth, bsz, 6, d)
    x2 = x.reshape(bsz * seq, d)
    final_gain = norm_final.reshape(1, d)
    for i in range(depth):
        x2 = _layer(x2, bsz, seq, mod[i], norm_mix[i], w_in[i], b_gate[i], conv_w[i], conv_b[i], a_log[i],
                    dt_bias[i], d_skip[i], ssm_norm[i], w_attn_br[i], w_ssm_br[i], w_out[i], norm_mlp[i],
                    w_mlp_in[i], w_mlp_out[i], final_gain, i == depth - 1)
    return x2.reshape(bsz, seq, d)
```

```python
import functools
import math

import numpy as np
import jax
import jax.numpy as jnp
from jax import lax
from jax.experimental import pallas as pl
from jax.experimental.pallas import tpu as pltpu

F32 = jnp.float32
BF16 = jnp.bfloat16

D_MODEL = 2048
ATTN_GROUPS = ((128, 1), (512, 4), (2048, 16))
ATTN_HEADS = 8
ATTN_HEAD_DIM = 128
ATTN_WIDTH = ATTN_HEADS * ATTN_HEAD_DIM
ATTN_HALF = 64
ALIBI_SLOPES = tuple(2.0 ** (-8.0 * (j + 1) / ATTN_HEADS) for j in range(ATTN_HEADS))
SSM_INNER = D_MODEL
SSM_HEAD_DIM = 64
SSM_HEADS = SSM_INNER // SSM_HEAD_DIM
SSM_GROUPS = 8
SSM_HEADS_PER_GROUP = SSM_HEADS // SSM_GROUPS
SSM_STATE = 128
SSM_CONV = 5
XBC_WIDTH = SSM_INNER + 2 * SSM_GROUPS * SSM_STATE
GROUP_WIDTH = SSM_HEADS_PER_GROUP * SSM_HEAD_DIM
MLP_HIDDEN = 4 * D_MODEL
NORM_EPS = 1e-6
QKV_GROUP_COLS = 3 * ATTN_WIDTH
QKV_COLS = len(ATTN_GROUPS) * QKV_GROUP_COLS
OFF_Z = QKV_COLS
OFF_XBC = OFF_Z + SSM_INNER
OFF_DT = OFF_XBC + XBC_WIDTH
OFF_GATE = OFF_DT + 2 * SSM_HEADS
IN_WIDTH = OFF_GATE + 2 * D_MODEL

LANES = 128
SSD_CHUNK = 128
CONV_HALO = 16
NEG_BIG = -1e30
LOG2E = math.log2(math.e)
LN2 = math.log(2.0)
VMEM_LIMIT = 56 * 1024 * 1024

NAT_XBC = 0
NAT_GATE = NAT_XBC + XBC_WIDTH
NAT_Z = NAT_GATE + 2 * D_MODEL
NAT_QKV = NAT_Z + SSM_INNER
NAT_WIDTH = NAT_QKV + QKV_GROUP_COLS


def _params(sem, vmem=VMEM_LIMIT):
    return pltpu.CompilerParams(dimension_semantics=sem, vmem_limit_bytes=vmem)


def _dot(a, b):
    return jnp.dot(a, b, preferred_element_type=F32)


def _dot_nt(a, b):
    return lax.dot_general(a, b, (((1,), (1,)), ((), ())), preferred_element_type=F32)


def _dot_exact(a, b):
    return jnp.dot(a, b, preferred_element_type=F32, precision=lax.Precision.HIGHEST)


def _resident(shape):
    return pl.BlockSpec(shape, lambda *_: (0,) * len(shape), pipeline_mode=pl.Buffered(1))


def _mod_kernel(ct_ref, w_ref, b_ref, o_ref):
    bsz = o_ref.shape[0]
    w = w_ref[...]
    rows = [jnp.sum(w * ct_ref[:, b:b + 1], axis=0, keepdims=True) for b in range(bsz)]
    o_ref[...] = jnp.concatenate(rows, axis=0) + b_ref[...]


def _modulation(c, w_mod, b_mod):
    depth, d, n = w_mod.shape
    bsz = c.shape[0]
    tn = 1024
    return pl.pallas_call(
        _mod_kernel,
        out_shape=jax.ShapeDtypeStruct((depth, bsz, n), F32),
        grid=(depth, n // tn),
        in_specs=[
            pl.BlockSpec((d, bsz), lambda l, j: (0, 0)),
            pl.BlockSpec((None, d, tn), lambda l, j: (l, 0, j)),
            pl.BlockSpec((None, 1, tn), lambda l, j: (l, 0, j)),
        ],
        out_specs=pl.BlockSpec((None, bsz, tn), lambda l, j: (l, 0, j)),
        compiler_params=_params(("parallel", "parallel")),
        name="adaln_mod",
    )(c.T, w_mod, b_mod.reshape(depth, 1, n))


_IN_TN = 1024
_NAT_TILES = NAT_WIDTH // _IN_TN
_QKV_TILES = QKV_GROUP_COLS // _IN_TN
_IN_TILES = _NAT_TILES + 2 * _QKV_TILES
_NORM_ROWS = 256


def _modulated_norm(x_ref, gain_ref, shift, scale, r0, rows):
    xf = x_ref[pl.ds(r0, rows), :]
    ms = jnp.mean(xf * xf, axis=-1, keepdims=True)
    y = xf * lax.rsqrt(ms + NORM_EPS) * gain_ref[...]
    return y * (1.0 + scale) + shift


def _inproj_kernel(x_ref, mod_ref, gain_ref, w_ref, wdt_ref, nat_ref, q1_ref, q2_ref, dt_ref, h_scr, r_scr):
    j = pl.program_id(1)
    tm = x_ref.shape[0]

    @pl.when(j == 0)
    def _():
        shift = mod_ref[0:1, :]
        scale = mod_ref[1:2, :]
        for r in range(tm // _NORM_ROWS):
            r0 = r * _NORM_ROWS
            h = _modulated_norm(x_ref, gain_ref, shift, scale, r0, _NORM_ROWS)
            h_hi = h.astype(BF16)
            h_lo = (h - h_hi.astype(F32)).astype(BF16)
            h_scr[pl.ds(r0, _NORM_ROWS), :] = h_hi
            both = _dot(h_hi, wdt_ref[...])
            dt_ref[pl.ds(r0, _NORM_ROWS), :] = (both[:, :LANES] + both[:, LANES:]
                                                + _dot(h_lo, wdt_ref[:, :LANES]))

    @pl.when(j < _NAT_TILES)
    def _():
        nat_ref[...] = _dot(h_scr[...], w_ref[...]).astype(BF16)

    for lo, ref in ((_NAT_TILES, q1_ref), (_NAT_TILES + _QKV_TILES, q2_ref)):
        @pl.when((j >= lo) & (j < lo + _QKV_TILES))
        def _(ref=ref):
            dil, rows = ref.shape[0], ref.shape[1]
            res = _dot(h_scr[...], w_ref[...])
            for cblk in range(_IN_TN // LANES):
                r_scr[cblk] = res[:, cblk * LANES:(cblk + 1) * LANES]
            for r in range(dil):
                for cblk in range(_IN_TN // LANES):
                    ref[r, :, cblk * LANES:(cblk + 1) * LANES] = (
                        r_scr[cblk, pl.ds(r, rows, stride=dil), :].astype(BF16))


def _in_projection(x2, mod_l, gain, w_main, w_dt, bsz):
    t, d = x2.shape
    seq = t // bsz
    tm = min(512, seq)
    tpb = seq // tm
    tn = _IN_TN
    d1, d2 = ATTN_GROUPS[1][1], ATTN_GROUPS[2][1]

    def dil_spec(dil, lo):
        return pl.BlockSpec((None, dil, tm // dil, tn),
                            lambda i, j: (i // tpb, 0, i % tpb, jnp.clip(j - lo, 0, _QKV_TILES - 1)))

    return pl.pallas_call(
        _inproj_kernel,
        out_shape=[jax.ShapeDtypeStruct((t, NAT_WIDTH), BF16),
                   jax.ShapeDtypeStruct((bsz, d1, seq // d1, QKV_GROUP_COLS), BF16),
                   jax.ShapeDtypeStruct((bsz, d2, seq // d2, QKV_GROUP_COLS), BF16),
                   jax.ShapeDtypeStruct((t, LANES), F32)],
        grid=(t // tm, _IN_TILES),
        in_specs=[
            pl.BlockSpec((tm, d), lambda i, j: (i, 0)),
            pl.BlockSpec((None, 6, d), lambda i, j: (i // tpb, 0, 0)),
            _resident((1, d)),
            pl.BlockSpec((None, d, tn), lambda i, j: (j, 0, 0)),
            _resident((d, 2 * LANES)),
        ],
        out_specs=[pl.BlockSpec((tm, tn), lambda i, j: (i, jnp.minimum(j, _NAT_TILES - 1))),
                   dil_spec(d1, _NAT_TILES),
                   dil_spec(d2, _NAT_TILES + _QKV_TILES),
                   pl.BlockSpec((tm, LANES), lambda i, j: (i, 0))],
        scratch_shapes=[pltpu.VMEM((tm, d), BF16), pltpu.VMEM((tn // LANES, tm, LANES), F32)],
        compiler_params=_params(("parallel", "arbitrary")),
        name="in_projection",
    )(x2, mod_l, gain, w_main, w_dt)


_ATTN_QB = 128


def _alibi_table(dil, qb):
    half = ATTN_HALF
    row = np.arange(qb)[:, None]
    col = np.arange(qb)[None, :]
    dist_c = np.abs(col - row)
    hcol = np.arange(2 * half)[None, :]
    dist_p = row + half - hcol
    dist_n = (hcol - half) + qb - row
    dist_h = np.where(hcol < half, dist_p, dist_n)
    dist = np.concatenate([dist_c, dist_h], axis=1).astype(np.float64)
    ok = dist <= half
    slopes = np.asarray(ALIBI_SLOPES, np.float64)[:, None, None]
    tbl = np.where(ok[None], -slopes * dist[None] * dil * LOG2E, NEG_BIG)
    return jnp.asarray(tbl, F32)


def _attn_kernel(q_ref, kc_ref, vc_ref, kp_ref, vp_ref, kn_ref, vn_ref, bias_ref, o_ref, lse_ref,
                 kh_scr, vh_scr, *, nblk):
    j = pl.program_id(2)
    qb = q_ref.shape[0]
    half = ATTN_HALF
    c1 = ATTN_HEAD_DIM ** -0.5 * LOG2E
    kh_scr[0:half, :] = kp_ref[...]
    kh_scr[half:2 * half, :] = kn_ref[...]
    vh_scr[0:half, :] = vp_ref[...]
    vh_scr[half:2 * half, :] = vn_ref[...]
    hcol = lax.broadcasted_iota(jnp.int32, (qb, 2 * half), 1)
    first_ok = jnp.where(j > 0, 0, half)
    end_ok = jnp.where(j < nblk - 1, 2 * half, half)
    edge_ok = (hcol >= first_ok) & (hcol < end_ok)
    lane = lax.broadcasted_iota(jnp.int32, (qb, LANES), 1)
    ones = jnp.ones((qb, LANES), BF16)
    lse_all = jnp.zeros((qb, LANES), F32)
    for h in range(ATTN_HEADS):
        sl = slice(h * ATTN_HEAD_DIM, (h + 1) * ATTN_HEAD_DIM)
        q = q_ref[:, sl]
        t_c = _dot_nt(q, kc_ref[:, sl]) * c1 + bias_ref[h, :, 0:qb]
        t_h = jnp.where(edge_ok, _dot_nt(q, kh_scr[:, sl]) * c1 + bias_ref[h, :, qb:], NEG_BIG)
        m = jnp.max(jnp.maximum(t_c, t_h), axis=-1, keepdims=True)
        p_c = jnp.exp2(t_c - m).astype(BF16)
        p_h = jnp.exp2(t_h - m).astype(BF16)
        acc = (_dot(p_c, jnp.concatenate([vc_ref[:, sl], ones], axis=1))
               + _dot(p_h, jnp.concatenate([vh_scr[:, sl], ones], axis=1)))
        l = acc[:, LANES:LANES + 1]
        o_ref[:, sl] = (acc[:, :LANES] * (1.0 / l)).astype(BF16)
        lse_all = jnp.where(lane == h, (m + jnp.log2(l)) * LN2, lse_all)
    lse_ref[...] = lse_all


def _attention_group(qkv, part0, dil):
    bsz, _, n, _ = qkv.shape
    qb = _ATTN_QB
    nblk = n // qb
    half = ATTN_HALF
    hb = qb // half
    nhalf = n // half
    w = ATTN_WIDTH
    cur = lambda part: pl.BlockSpec((None, None, qb, w), lambda b, r, j: (b, r, j, part0 + part))
    prev = lambda part: pl.BlockSpec(
        (None, None, half, w), lambda b, r, j: (b, r, jnp.maximum(j * hb - 1, 0), part0 + part))
    nxt = lambda part: pl.BlockSpec(
        (None, None, half, w), lambda b, r, j: (b, r, jnp.minimum((j + 1) * hb, nhalf - 1), part0 + part))
    return pl.pallas_call(
        functools.partial(_attn_kernel, nblk=nblk),
        out_shape=[jax.ShapeDtypeStruct((bsz, dil, n, w), BF16),
                   jax.ShapeDtypeStruct((bsz, dil, n, LANES), F32)],
        grid=(bsz, dil, nblk),
        in_specs=[cur(0), cur(1), cur(2), prev(1), prev(2), nxt(1), nxt(2),
                  _resident((ATTN_HEADS, qb, 2 * qb))],
        out_specs=[pl.BlockSpec((None, None, qb, w), lambda b, r, j: (b, r, j, 0)),
                   pl.BlockSpec((None, None, qb, LANES), lambda b, r, j: (b, r, j, 0))],
        scratch_shapes=[pltpu.VMEM((2 * half, w), BF16), pltpu.VMEM((2 * half, w), BF16)],
        compiler_params=_params(("parallel", "parallel", "parallel")),
        name=f"band_attention_d{dil}",
    )(qkv, qkv, qkv, qkv, qkv, qkv, qkv, _alibi_table(dil, qb))


def _softplus(x):
    return jnp.maximum(x, 0.0) + jnp.log(1.0 + jnp.exp(-jnp.abs(x)))


def _head_expand(cols, lane):
    out = cols[-1]
    for r in range(SSM_HEADS_PER_GROUP - 2, -1, -1):
        out = jnp.where(lane < (r + 1) * SSM_HEAD_DIM, cols[r], out)
    return out


def _dt_terms(dt_ref, par_ref):
    a_row = -jnp.exp(par_ref[0:1, :])
    dtv = _softplus(dt_ref[...] + par_ref[1:2, :])
    return dtv, dtv * a_row


def _ssd_fwd_kernel(xm_ref, xp_ref, xn_ref, dt_ref, cw_ref, cb_ref, par_ref, dsk_ref,
                    xc_ref, y1_ref, xf_scr, st_scr, *, nc):
    c = pl.program_id(1)
    L = SSD_CHUNK
    H = SSM_HEADS
    R = SSM_HEADS_PER_GROUP
    N = SSM_STATE
    GW = GROUP_WIDTH

    @pl.when(c == 0)
    def _():
        st_scr[...] = jnp.zeros_like(st_scr)

    has_prev = c > 0
    has_next = c < nc - 1
    cwid = 512
    for cc in range(XBC_WIDTH // cwid):
        cols = slice(cc * cwid, (cc + 1) * cwid)
        xm = xm_ref[:, cols].astype(F32)
        xp = jnp.where(has_prev, xp_ref[:, cols].astype(F32), 0.0)
        xn = jnp.where(has_next, xn_ref[:, cols].astype(F32), 0.0)
        xcat = jnp.concatenate([xp, xm, xn], axis=0)
        acc = cb_ref[:, cols] + cw_ref[0:1, cols] * xcat[CONV_HALO - 2:CONV_HALO - 2 + L]
        for k in range(1, SSM_CONV):
            off = CONV_HALO - 2 + k
            acc = acc + cw_ref[k:k + 1, cols] * xcat[off:off + L]
        sv = acc * (1.0 / (1.0 + jnp.exp(-acc)))
        xf_scr[:, cols] = sv
        xc_ref[:, cols] = sv.astype(BF16)

    dtv, dta = _dt_terms(dt_ref, par_ref)
    ti = lax.broadcasted_iota(jnp.int32, (L, L), 0)
    si = lax.broadcasted_iota(jnp.int32, (L, L), 1)
    lower = si <= ti
    strict_lower = si < ti
    strict_upper = si > ti
    a_f = _dot_exact(lower.astype(F32), dta)
    g_b = _dot_exact((si >= ti).astype(F32), dta)
    a_f_t = a_f.T
    g_b_t = g_b.T
    dt_t = dtv.T
    exp_af = jnp.exp(a_f)
    a_end = a_f[L - 1:L, :]
    w_end = jnp.exp(a_end - a_f) * dtv
    cd_row = jnp.exp(a_end)
    lane = lax.broadcasted_iota(jnp.int32, (L, GW), 1)
    lane_n = lax.broadcasted_iota(jnp.int32, (N, GW), 1)

    for g in range(SSM_GROUPS):
        gsl = slice(g * GW, (g + 1) * GW)
        xg = xf_scr[:, gsl]
        xg16 = xg.astype(BF16)
        bg = xf_scr[:, SSM_INNER + g * N:SSM_INNER + (g + 1) * N]
        cg16 = xf_scr[:, SSM_INNER + SSM_GROUPS * N + g * N:
                      SSM_INNER + SSM_GROUPS * N + (g + 1) * N].astype(BF16)
        bg16 = bg.astype(BF16)
        cb = _dot_nt(cg16, bg16)
        ms, xs = [], []
        for r in range(R):
            h = g * R + r
            e = jnp.where(lower, a_f[:, h:h + 1] - a_f_t[h:h + 1, :],
                          g_b[:, H + h:H + h + 1] - g_b_t[H + h:H + h + 1, :])
            dtf_row = dt_t[h:h + 1, :]
            dtb_row = dt_t[H + h:H + h + 1, :]
            dsel = jnp.where(strict_lower, dtf_row, jnp.where(strict_upper, dtb_row, dtf_row + dtb_row))
            ms.append((cb * (jnp.exp(e) * dsel)).astype(BF16))
            xs.append(jnp.where((lane >= r * SSM_HEAD_DIM) & (lane < (r + 1) * SSM_HEAD_DIM),
                                xg16, jnp.zeros_like(xg16)))
        y = _dot(jnp.concatenate(ms, axis=1), jnp.concatenate(xs, axis=0))
        st = st_scr[g]
        y = y + _dot(cg16, st.astype(BF16)) * _head_expand(
            [exp_af[:, g * R + r:g * R + r + 1] for r in range(R)], lane)
        y = y + xg * dsk_ref[:, gsl]
        y1_ref[:, gsl] = y.astype(BF16)
        xw = (xg * _head_expand([w_end[:, g * R + r:g * R + r + 1] for r in range(R)], lane)).astype(BF16)
        s_new = _dot(bg.T.astype(BF16), xw)
        st_scr[g] = st * _head_expand([cd_row[:, g * R + r:g * R + r + 1] for r in range(R)], lane_n) + s_new


def _ssd_bwd_kernel(xc_ref, dt_ref, par_ref, y1_ref, z_ref, ng_ref, y_ref, st_scr):
    c = pl.program_id(1)
    L = SSD_CHUNK
    H = SSM_HEADS
    R = SSM_HEADS_PER_GROUP
    N = SSM_STATE
    GW = GROUP_WIDTH

    @pl.when(c == 0)
    def _():
        st_scr[...] = jnp.zeros_like(st_scr)

    dtv, dta = _dt_terms(dt_ref, par_ref)
    ti = lax.broadcasted_iota(jnp.int32, (L, L), 0)
    si = lax.broadcasted_iota(jnp.int32, (L, L), 1)
    g_b = _dot_exact((si >= ti).astype(F32), dta)
    exp_gb = jnp.exp(g_b)
    g_start = g_b[0:1, :]
    w_start = jnp.exp(g_start - g_b) * dtv
    cd_row = jnp.exp(g_start)
    lane = lax.broadcasted_iota(jnp.int32, (L, GW), 1)
    lane_n = lax.broadcasted_iota(jnp.int32, (N, GW), 1)

    for g in range(SSM_GROUPS):
        gsl = slice(g * GW, (g + 1) * GW)
        xg = xc_ref[:, gsl].astype(F32)
        bg = xc_ref[:, SSM_INNER + g * N:SSM_INNER + (g + 1) * N].astype(F32)
        cg16 = xc_ref[:, SSM_INNER + SSM_GROUPS * N + g * N:SSM_INNER + SSM_GROUPS * N + (g + 1) * N]
        st = st_scr[g]
        heads = [H + g * R + r for r in range(R)]
        y = y1_ref[:, gsl].astype(F32) + _dot(cg16, st.astype(BF16)) * _head_expand(
            [exp_gb[:, h:h + 1] for h in heads], lane)
        zg = z_ref[:, gsl].astype(F32)
        y = y * (zg * (1.0 / (1.0 + jnp.exp(-zg))))
        ms = jnp.mean(y * y, axis=-1, keepdims=True)
        y_ref[:, gsl] = (y * lax.rsqrt(ms + NORM_EPS) * ng_ref[:, gsl]).astype(BF16)
        xw = (xg * _head_expand([w_start[:, h:h + 1] for h in heads], lane)).astype(BF16)
        s_new = _dot(bg.T.astype(BF16), xw)
        st_scr[g] = st * _head_expand([cd_row[:, h:h + 1] for h in heads], lane_n) + s_new


def _bidirectional_ssd(nat, dt, conv_w, conv_b, ssm_par, d_skip_x, norm_g, bsz, seq):
    L = SSD_CHUNK
    nc = seq // L
    hpc = L // CONV_HALO
    nhalo = seq // CONV_HALO
    nat3 = nat.reshape(bsz, seq, NAT_WIDTH)
    dt3 = dt.reshape(bsz, seq, LANES)
    xbc_blk = NAT_XBC // XBC_WIDTH
    z_blk = NAT_Z // SSM_INNER
    state = pltpu.VMEM((SSM_GROUPS, SSM_STATE, GROUP_WIDTH), F32)
    xc, y1 = pl.pallas_call(
        functools.partial(_ssd_fwd_kernel, nc=nc),
        out_shape=[jax.ShapeDtypeStruct((bsz, seq, XBC_WIDTH), BF16),
                   jax.ShapeDtypeStruct((bsz, seq, SSM_INNER), BF16)],
        grid=(bsz, nc),
        in_specs=[
            pl.BlockSpec((None, L, XBC_WIDTH), lambda b, c: (b, c, xbc_blk)),
            pl.BlockSpec((None, CONV_HALO, XBC_WIDTH), lambda b, c: (b, jnp.maximum(c * hpc - 1, 0), xbc_blk)),
            pl.BlockSpec((None, CONV_HALO, XBC_WIDTH),
                         lambda b, c: (b, jnp.minimum((c + 1) * hpc, nhalo - 1), xbc_blk)),
            pl.BlockSpec((None, L, LANES), lambda b, c: (b, c, 0)),
            _resident((SSM_CONV, XBC_WIDTH)), _resident((1, XBC_WIDTH)), _resident((8, LANES)),
            _resident((1, SSM_INNER)),
        ],
        out_specs=[pl.BlockSpec((None, L, XBC_WIDTH), lambda b, c: (b, c, 0)),
                   pl.BlockSpec((None, L, SSM_INNER), lambda b, c: (b, c, 0))],
        scratch_shapes=[pltpu.VMEM((L, XBC_WIDTH), F32), state],
        compiler_params=_params(("parallel", "arbitrary")),
        name="ssd_forward_sweep",
    )(nat3, nat3, nat3, dt3, conv_w, conv_b, ssm_par, d_skip_x)
    rev = lambda width, blk=0: pl.BlockSpec((None, L, width), lambda b, c: (b, nc - 1 - c, blk))
    y = pl.pallas_call(
        _ssd_bwd_kernel,
        out_shape=jax.ShapeDtypeStruct((bsz, seq, SSM_INNER), BF16),
        grid=(bsz, nc),
        in_specs=[rev(XBC_WIDTH), rev(LANES), _resident((8, LANES)), rev(SSM_INNER),
                  rev(SSM_INNER, z_blk), _resident((1, SSM_INNER))],
        out_specs=rev(SSM_INNER),
        scratch_shapes=[state],
        compiler_params=_params(("parallel", "arbitrary")),
        name="ssd_backward_sweep",
    )(xc, dt3, ssm_par, y1, nat3, norm_g)
    return y.reshape(bsz * seq, SSM_INNER)


_MERGE_TM = 512
_MERGE_NB = 512
_MERGE_PB = 256


def _interleave_matrix(tm, dil):
    p = np.zeros((tm, tm), np.float32)
    t = np.arange(tm)
    p[t, (t % dil) * (tm // dil) + t // dil] = 1.0
    return jnp.asarray(p, BF16)


def _split3(x):
    hi = x.astype(BF16)
    r1 = x - hi.astype(F32)
    mid = r1.astype(BF16)
    lo = (r1 - mid.astype(F32)).astype(BF16)
    return hi, mid, lo


def _merge_kernel(o0_ref, o1_ref, o2_ref, l0_ref, l1_ref, l2_ref, p1_ref, p2_ref, ys_ref, gate_ref, bg_ref,
                  wa_ref, ws_ref, out_ref, ya_scr):
    tm, d = out_ref.shape

    def token_order(o_ref, l_ref, p_ref):
        dil = o_ref.shape[0]
        pb = p_ref.shape[0]
        rpb = pb // dil
        w = o_ref.shape[2]
        o_tok, l_tok = [], []
        for blk in range(tm // pb):
            rs = slice(blk * rpb, (blk + 1) * rpb)
            flat = jnp.concatenate(
                [jnp.concatenate((o_ref[r, rs, :],) + _split3(l_ref[r, rs, :]), axis=1) for r in range(dil)],
                axis=0)
            res = _dot(p_ref[...], flat)
            o_tok.append(res[:, :w])
            l_tok.append(res[:, w:w + LANES] + res[:, w + LANES:w + 2 * LANES] + res[:, w + 2 * LANES:])
        return jnp.concatenate(o_tok, axis=0), jnp.concatenate(l_tok, axis=0)

    o1, l1 = token_order(o1_ref, l1_ref, p1_ref)
    o2, l2 = token_order(o2_ref, l2_ref, p2_ref)
    l0 = l0_ref[...]
    m = jnp.maximum(l0, jnp.maximum(l1, l2))
    e0, e1, e2 = jnp.exp(l0 - m), jnp.exp(l1 - m), jnp.exp(l2 - m)
    inv = 1.0 / (e0 + e1 + e2)
    w0, w1, w2 = e0 * inv, e1 * inv, e2 * inv
    for h in range(ATTN_HEADS):
        sl = slice(h * ATTN_HEAD_DIM, (h + 1) * ATTN_HEAD_DIM)
        ya_scr[:, sl] = (w0[:, h:h + 1] * o0_ref[:, sl].astype(F32) + w1[:, h:h + 1] * o1[:, sl]
                         + w2[:, h:h + 1] * o2[:, sl]).astype(BF16)
    nb = _MERGE_NB
    for cb in range(d // nb):
        cs = slice(cb * nb, (cb + 1) * nb)
        cs2 = slice(d + cb * nb, d + (cb + 1) * nb)
        ga = gate_ref[:, cs].astype(F32) + bg_ref[:, cs]
        gs = gate_ref[:, cs2].astype(F32) + bg_ref[:, cs2]
        a = _dot(ya_scr[...], wa_ref[:, cs])
        s = _dot(ys_ref[...], ws_ref[:, cs])
        out_ref[:, cs] = (a * (1.0 / (1.0 + jnp.exp(-ga))) + s * (1.0 / (1.0 + jnp.exp(-gs)))).astype(BF16)


def _branch_merge(o0, o1, o2, l0, l1, l2, y_ssm, nat, b_gate, w_attn_br, w_ssm_br, bsz):
    t = y_ssm.shape[0]
    d = D_MODEL
    seq = t // bsz
    tm = min(_MERGE_TM, seq // ATTN_GROUPS[2][1])
    tpb = seq // tm
    pb = min(_MERGE_PB, tm)
    row = lambda width, blk=0: pl.BlockSpec((tm, width), lambda i: (i, blk))

    def dil_spec(arr):
        dil, width = arr.shape[1], arr.shape[3]
        return pl.BlockSpec((None, dil, tm // dil, width), lambda i: (i // tpb, 0, i % tpb, 0))

    return pl.pallas_call(
        _merge_kernel,
        out_shape=jax.ShapeDtypeStruct((t, d), BF16),
        grid=(t // tm,),
        in_specs=[row(ATTN_WIDTH), dil_spec(o1), dil_spec(o2), row(LANES), dil_spec(l1), dil_spec(l2),
                  _resident((pb, pb)), _resident((pb, pb)),
                  row(SSM_INNER), row(2 * d, NAT_GATE // (2 * d)), _resident((1, 2 * d)),
                  _resident((ATTN_WIDTH, d)), _resident((SSM_INNER, d))],
        out_specs=row(d),
        scratch_shapes=[pltpu.VMEM((tm, ATTN_WIDTH), BF16)],
        compiler_params=_params(("parallel",)),
        name="branch_merge",
    )(o0, o1, o2, l0, l1, l2, _interleave_matrix(pb, o1.shape[1]), _interleave_matrix(pb, o2.shape[1]),
      y_ssm, nat, b_gate, w_attn_br, w_ssm_br)


def _outproj_kernel(mg_ref, w_ref, x_ref, mod_ref, gain_ref, xo_ref, h2_ref):
    xn = x_ref[...] + mod_ref[2:3, :] * _dot(mg_ref[...], w_ref[...])
    xo_ref[...] = xn
    ms = jnp.mean(xn * xn, axis=-1, keepdims=True)
    y = xn * lax.rsqrt(ms + NORM_EPS) * gain_ref[...]
    h2_ref[...] = (y * (1.0 + mod_ref[4:5, :]) + mod_ref[3:4, :]).astype(BF16)


def _out_projection(merged, w_out, x2, mod_l, gain, bsz):
    t, d = x2.shape
    seq = t // bsz
    tm = min(512, seq)
    tpb = seq // tm
    row = lambda: pl.BlockSpec((tm, d), lambda i: (i, 0))
    return pl.pallas_call(
        _outproj_kernel,
        out_shape=[jax.ShapeDtypeStruct((t, d), F32), jax.ShapeDtypeStruct((t, d), BF16)],
        grid=(t // tm,),
        in_specs=[row(), _resident((d, d)), row(),
                  pl.BlockSpec((None, 6, d), lambda i: (i // tpb, 0, 0)),
                  _resident((1, d))],
        out_specs=[row(), row()],
        compiler_params=_params(("parallel",)),
        name="out_projection",
    )(merged, w_out, x2, mod_l, gain)


_MLP_TK = 1024


def _mlp_kernel(h_ref, w1_ref, w2_ref, x_ref, mod_ref, fg_ref, o_ref, *, nk, final_norm):
    k = pl.program_id(1)
    u = jnp.maximum(_dot(h_ref[...], w1_ref[...]), 0.0)
    part = _dot((u * u).astype(BF16), w2_ref[...])

    @pl.when(k == 0)
    def _():
        o_ref[...] = part

    @pl.when(k > 0)
    def _():
        o_ref[...] += part

    @pl.when(k == nk - 1)
    def _():
        xn = x_ref[...] + mod_ref[5:6, :] * o_ref[...]
        if final_norm:
            ms = jnp.mean(xn * xn, axis=-1, keepdims=True)
            xn = xn * lax.rsqrt(ms + NORM_EPS) * fg_ref[...]
        o_ref[...] = xn


def _mlp(h2, w1_blocks, w2, x2, mod_l, final_gain, bsz, final_norm):
    t, d = x2.shape
    seq = t // bsz
    tm = min(512, seq)
    nk, _, tk = w1_blocks.shape
    tpb = seq // tm
    return pl.pallas_call(
        functools.partial(_mlp_kernel, nk=nk, final_norm=final_norm),
        out_shape=jax.ShapeDtypeStruct((t, d), F32),
        grid=(t // tm, nk),
        in_specs=[pl.BlockSpec((tm, d), lambda i, k: (i, 0)),
                  pl.BlockSpec((None, d, tk), lambda i, k: (k, 0, 0)),
                  pl.BlockSpec((tk, d), lambda i, k: (k, 0)),
                  pl.BlockSpec((tm, d), lambda i, k: (i, 0)),
                  pl.BlockSpec((None, 6, d), lambda i, k: (i // tpb, 0, 0)),
                  _resident((1, d))],
        out_specs=pl.BlockSpec((tm, d), lambda i, k: (i, 0)),
        compiler_params=_params(("parallel", "arbitrary")),
        name="relu2_mlp",
    )(h2, w1_blocks, w2, x2, mod_l, final_gain)


def _column_blocks(w, tn):
    k, n = w.shape
    return w.reshape(k, n // tn, tn).transpose(1, 0, 2)


def _layer(x2, bsz, seq, mod_l, norm_mix, w_in, b_gate, conv_w, conv_b, a_log, dt_bias, d_skip, ssm_norm,
           w_attn_br, w_ssm_br, w_out, norm_mlp, w_mlp_in, w_mlp_out, final_gain, final_norm):
    d = D_MODEL
    g0, g1, g2 = (w_in[:, g * QKV_GROUP_COLS:(g + 1) * QKV_GROUP_COLS] for g in range(3))
    w_main = jnp.concatenate([w_in[:, OFF_XBC:OFF_DT], w_in[:, OFF_GATE:], w_in[:, OFF_Z:OFF_XBC], g0, g1, g2],
                             axis=1).astype(BF16)
    w_main = _column_blocks(w_main, _IN_TN)
    w_dt = jnp.pad(w_in[:, OFF_DT:OFF_GATE], ((0, 0), (0, LANES - 2 * SSM_HEADS)))
    w_dt_hi = w_dt.astype(BF16)
    w_dt_lo = (w_dt - w_dt_hi.astype(F32)).astype(BF16)
    w_dt2 = jnp.concatenate([w_dt_hi, w_dt_lo], axis=1)
    ssm_par = jnp.pad(jnp.stack([a_log.reshape(-1), dt_bias.reshape(-1)]),
                      ((0, 6), (0, LANES - 2 * SSM_HEADS)))
    d_skip_x = jnp.repeat(d_skip, SSM_HEAD_DIM).reshape(1, SSM_INNER)

    nat, q1, q2, dt = _in_projection(x2, mod_l, norm_mix.reshape(1, d), w_main, w_dt2, bsz)
    nat4 = nat.reshape(bsz, 1, seq, NAT_WIDTH)
    o0, l0 = _attention_group(nat4, NAT_QKV // ATTN_WIDTH, 1)
    o1, l1 = _attention_group(q1, 0, ATTN_GROUPS[1][1])
    o2, l2 = _attention_group(q2, 0, ATTN_GROUPS[2][1])
    y_ssm = _bidirectional_ssd(nat, dt, conv_w.reshape(SSM_CONV, XBC_WIDTH), conv_b.reshape(1, XBC_WIDTH),
                               ssm_par, d_skip_x, ssm_norm.reshape(1, SSM_INNER), bsz, seq)
    merged = _branch_merge(o0.reshape(bsz * seq, ATTN_WIDTH), o1, o2, l0.reshape(bsz * seq, LANES), l1, l2,
                           y_ssm, nat, b_gate.reshape(1, 2 * d),
                           w_attn_br.astype(BF16), w_ssm_br.astype(BF16), bsz)
    x2, h2 = _out_projection(merged, w_out.astype(BF16), x2, mod_l, norm_mlp.reshape(1, d), bsz)
    return _mlp(h2, _column_blocks(w_mlp_in.astype(BF16), _MLP_TK), w_mlp_out.astype(BF16), x2, mod_l,
                final_gain, bsz, final_norm)


def kernel(x, c, w_mod, b_mod, norm_mix, w_in, b_gate, conv_w, conv_b, a_log, dt_bias, d_skip, ssm_norm,
           w_attn_br, w_ssm_br, w_out, norm_mlp, w_mlp_in, w_mlp_out, norm_final):
    bsz, seq, d = x.shape
    depth = w_mod.shape[0]
    mod = _modulation(c, w_mod, b_mod).reshape(depth, bsz, 6, d)
    x2 = x.reshape(bsz * seq, d)
    final_gain = norm_final.reshape(1, d)
    for i in range(depth):
        x2 = _layer(x2, bsz, seq, mod[i], norm_mix[i], w_in[i], b_gate[i], conv_w[i], conv_b[i], a_log[i],
                    dt_bias[i], d_skip[i], ssm_norm[i], w_attn_br[i], w_ssm_br[i], w_out[i], norm_mlp[i],
                    w_mlp_in[i], w_mlp_out[i], final_gain, i == depth - 1)
    return x2.reshape(bsz, seq, d)
```

```python
import functools
import math

import numpy as np
import jax
import jax.numpy as jnp
from jax import lax
from jax.experimental import pallas as pl
from jax.experimental.pallas import tpu as pltpu

F32 = jnp.float32
BF16 = jnp.bfloat16

D_MODEL = 2048
ATTN_GROUPS = ((128, 1), (512, 4), (2048, 16))
ATTN_HEADS = 8
ATTN_HEAD_DIM = 128
ATTN_WIDTH = ATTN_HEADS * ATTN_HEAD_DIM
ATTN_HALF = 64
ALIBI_SLOPES = tuple(2.0 ** (-8.0 * (j + 1) / ATTN_HEADS) for j in range(ATTN_HEADS))
SSM_INNER = D_MODEL
SSM_HEAD_DIM = 64
SSM_HEADS = SSM_INNER // SSM_HEAD_DIM
SSM_GROUPS = 8
SSM_HEADS_PER_GROUP = SSM_HEADS // SSM_GROUPS
SSM_STATE = 128
SSM_CONV = 5
XBC_WIDTH = SSM_INNER + 2 * SSM_GROUPS * SSM_STATE
GROUP_WIDTH = SSM_HEADS_PER_GROUP * SSM_HEAD_DIM
MLP_HIDDEN = 4 * D_MODEL
NORM_EPS = 1e-6
QKV_GROUP_COLS = 3 * ATTN_WIDTH
QKV_COLS = len(ATTN_GROUPS) * QKV_GROUP_COLS
OFF_Z = QKV_COLS
OFF_XBC = OFF_Z + SSM_INNER
OFF_DT = OFF_XBC + XBC_WIDTH
OFF_GATE = OFF_DT + 2 * SSM_HEADS
IN_WIDTH = OFF_GATE + 2 * D_MODEL

LANES = 128
SSD_CHUNK = 128
CONV_HALO = 16
NEG_BIG = -1e30
LOG2E = math.log2(math.e)
LN2 = math.log(2.0)
VMEM_LIMIT = 56 * 1024 * 1024

NAT_XBC = 0
NAT_GATE = NAT_XBC + XBC_WIDTH
NAT_Z = NAT_GATE + 2 * D_MODEL
NAT_QKV = NAT_Z + SSM_INNER
NAT_WIDTH = NAT_QKV + QKV_GROUP_COLS


def _params(sem, vmem=VMEM_LIMIT):
    return pltpu.CompilerParams(dimension_semantics=sem, vmem_limit_bytes=vmem)


def _dot(a, b):
    return jnp.dot(a, b, preferred_element_type=F32)


def _dot_nt(a, b):
    return lax.dot_general(a, b, (((1,), (1,)), ((), ())), preferred_element_type=F32)


def _dot_tn(a, b):
    return lax.dot_general(a, b, (((0,), (0,)), ((), ())), preferred_element_type=F32)


def _dot_exact(a, b):
    return jnp.dot(a, b, preferred_element_type=F32, precision=lax.Precision.HIGHEST)


def _resident(shape):
    return pl.BlockSpec(shape, lambda *_: (0,) * len(shape), pipeline_mode=pl.Buffered(1))


def _mod_kernel(ct_ref, w_ref, b_ref, o_ref):
    bsz = o_ref.shape[0]
    w = w_ref[...]
    rows = [jnp.sum(w * ct_ref[:, b:b + 1], axis=0, keepdims=True) for b in range(bsz)]
    o_ref[...] = jnp.concatenate(rows, axis=0) + b_ref[...]


def _modulation(c, w_mod, b_mod):
    depth, d, n = w_mod.shape
    bsz = c.shape[0]
    tn = 1024
    return pl.pallas_call(
        _mod_kernel,
        out_shape=jax.ShapeDtypeStruct((depth, bsz, n), F32),
        grid=(depth, n // tn),
        in_specs=[
            pl.BlockSpec((d, bsz), lambda l, j: (0, 0)),
            pl.BlockSpec((None, d, tn), lambda l, j: (l, 0, j)),
            pl.BlockSpec((None, 1, tn), lambda l, j: (l, 0, j)),
        ],
        out_specs=pl.BlockSpec((None, bsz, tn), lambda l, j: (l, 0, j)),
        compiler_params=_params(("parallel", "parallel")),
        name="adaln_mod",
    )(c.T, w_mod, b_mod.reshape(depth, 1, n))


_IN_TM = 1024
_IN_TN = 1024
_NAT_TILES = NAT_WIDTH // _IN_TN
_QKV_TILES = QKV_GROUP_COLS // _IN_TN
_IN_TILES = _NAT_TILES + 2 * _QKV_TILES
_NORM_ROWS = 256


def _modulated_norm(x_ref, gain_ref, shift, scale, r0, rows):
    xf = x_ref[pl.ds(r0, rows), :]
    ms = jnp.mean(xf * xf, axis=-1, keepdims=True)
    y = xf * lax.rsqrt(ms + NORM_EPS) * gain_ref[...]
    return y * (1.0 + scale) + shift


def _inproj_kernel(x_ref, mod_ref, gain_ref, w_ref, wdt_ref, nat_ref, q1_ref, q2_ref, dt_ref, h_scr, r_scr):
    j = pl.program_id(1)
    tm = x_ref.shape[0]

    @pl.when(j == 0)
    def _():
        shift = mod_ref[0:1, :]
        scale = mod_ref[1:2, :]
        for r in range(tm // _NORM_ROWS):
            r0 = r * _NORM_ROWS
            h = _modulated_norm(x_ref, gain_ref, shift, scale, r0, _NORM_ROWS)
            h_hi = h.astype(BF16)
            h_lo = (h - h_hi.astype(F32)).astype(BF16)
            h_scr[pl.ds(r0, _NORM_ROWS), :] = h_hi
            both = _dot(h_hi, wdt_ref[...])
            dt_ref[pl.ds(r0, _NORM_ROWS), :] = (both[:, :LANES] + both[:, LANES:]
                                                + _dot(h_lo, wdt_ref[:, :LANES]))

    @pl.when(j < _NAT_TILES)
    def _():
        nat_ref[...] = _dot(h_scr[...], w_ref[...]).astype(BF16)

    for lo, ref in ((_NAT_TILES, q1_ref), (_NAT_TILES + _QKV_TILES, q2_ref)):
        @pl.when((j >= lo) & (j < lo + _QKV_TILES))
        def _(ref=ref):
            dil, rows = ref.shape[0], ref.shape[1]
            res = _dot(h_scr[...], w_ref[...])
            for cblk in range(_IN_TN // LANES):
                r_scr[cblk] = res[:, cblk * LANES:(cblk + 1) * LANES]
            for r in range(dil):
                for cblk in range(_IN_TN // LANES):
                    ref[r, :, cblk * LANES:(cblk + 1) * LANES] = (
                        r_scr[cblk, pl.ds(r, rows, stride=dil), :].astype(BF16))


def _in_projection(x2, mod_l, gain, w_main, w_dt, bsz):
    t, d = x2.shape
    seq = t // bsz
    tm = min(_IN_TM, seq)
    tpb = seq // tm
    tn = _IN_TN
    d1, d2 = ATTN_GROUPS[1][1], ATTN_GROUPS[2][1]

    def dil_spec(dil, lo):
        return pl.BlockSpec((None, dil, tm // dil, tn),
                            lambda i, j: (i // tpb, 0, i % tpb, jnp.clip(j - lo, 0, _QKV_TILES - 1)))

    return pl.pallas_call(
        _inproj_kernel,
        out_shape=[jax.ShapeDtypeStruct((t, NAT_WIDTH), BF16),
                   jax.ShapeDtypeStruct((bsz, d1, seq // d1, QKV_GROUP_COLS), BF16),
                   jax.ShapeDtypeStruct((bsz, d2, seq // d2, QKV_GROUP_COLS), BF16),
                   jax.ShapeDtypeStruct((t, LANES), F32)],
        grid=(t // tm, _IN_TILES),
        in_specs=[
            pl.BlockSpec((tm, d), lambda i, j: (i, 0)),
            pl.BlockSpec((None, 6, d), lambda i, j: (i // tpb, 0, 0)),
            _resident((1, d)),
            pl.BlockSpec((d, tn), lambda i, j: (0, j)),
            _resident((d, 2 * LANES)),
        ],
        out_specs=[pl.BlockSpec((tm, tn), lambda i, j: (i, jnp.minimum(j, _NAT_TILES - 1))),
                   dil_spec(d1, _NAT_TILES),
                   dil_spec(d2, _NAT_TILES + _QKV_TILES),
                   pl.BlockSpec((tm, LANES), lambda i, j: (i, 0))],
        scratch_shapes=[pltpu.VMEM((tm, d), BF16), pltpu.VMEM((tn // LANES, tm, LANES), F32)],
        compiler_params=_params(("parallel", "arbitrary")),
        name="in_projection",
    )(x2, mod_l, gain, w_main, w_dt)


_ATTN_QB = 128
_ATTN_NSUB = 4


def _alibi_table(dil, qb):
    half = ATTN_HALF
    row = np.arange(qb)[:, None]
    col = np.arange(qb)[None, :]
    dist_c = np.abs(col - row)
    hcol = np.arange(2 * half)[None, :]
    dist_p = row + half - hcol
    dist_n = (hcol - half) + qb - row
    dist_h = np.where(hcol < half, dist_p, dist_n)
    dist = np.concatenate([dist_c, dist_h], axis=1).astype(np.float64)
    ok = dist <= half
    slopes = np.asarray(ALIBI_SLOPES, np.float64)[:, None, None]
    tbl = np.where(ok[None], -slopes * dist[None] * dil * LOG2E, NEG_BIG)
    return jnp.asarray(tbl, F32)


def _attn_kernel(q_ref, kc_ref, vc_ref, kp_ref, vp_ref, kn_ref, vn_ref, bias_ref, o_ref, lse_ref,
                 kh_scr, vh_scr, *, nblk):
    j = pl.program_id(2)
    qb = _ATTN_QB
    nsub = q_ref.shape[0] // qb
    half = ATTN_HALF
    c1 = ATTN_HEAD_DIM ** -0.5 * LOG2E
    hcol = lax.broadcasted_iota(jnp.int32, (qb, 2 * half), 1)
    lane = lax.broadcasted_iota(jnp.int32, (qb, LANES), 1)
    ones = jnp.ones((qb, LANES), BF16)
    for s in range(nsub):
        rows = pl.ds(s * qb, qb)
        edge_ok = None
        if s == 0:
            kh_scr[s, 0:half, :] = kp_ref[...]
            vh_scr[s, 0:half, :] = vp_ref[...]
            edge_ok = hcol >= jnp.where(j > 0, 0, half)
        else:
            kh_scr[s, 0:half, :] = kc_ref[pl.ds(s * qb - half, half), :]
            vh_scr[s, 0:half, :] = vc_ref[pl.ds(s * qb - half, half), :]
        if s == nsub - 1:
            kh_scr[s, half:2 * half, :] = kn_ref[...]
            vh_scr[s, half:2 * half, :] = vn_ref[...]
            ok_n = hcol < jnp.where(j < nblk - 1, 2 * half, half)
            edge_ok = ok_n if edge_ok is None else edge_ok & ok_n
        else:
            kh_scr[s, half:2 * half, :] = kc_ref[pl.ds((s + 1) * qb, half), :]
            vh_scr[s, half:2 * half, :] = vc_ref[pl.ds((s + 1) * qb, half), :]
        lse_all = jnp.zeros((qb, LANES), F32)
        for h in range(ATTN_HEADS):
            sl = slice(h * ATTN_HEAD_DIM, (h + 1) * ATTN_HEAD_DIM)
            q = q_ref[rows, sl]
            t_c = _dot_nt(q, kc_ref[rows, sl]) * c1 + bias_ref[h, :, 0:qb]
            t_h = _dot_nt(q, kh_scr[s, :, sl]) * c1 + bias_ref[h, :, qb:]
            if edge_ok is not None:
                t_h = jnp.where(edge_ok, t_h, NEG_BIG)
            m = jnp.max(jnp.maximum(t_c, t_h), axis=-1, keepdims=True)
            p_c = jnp.exp2(t_c - m).astype(BF16)
            p_h = jnp.exp2(t_h - m).astype(BF16)
            acc = (_dot(p_c, jnp.concatenate([vc_ref[rows, sl], ones], axis=1))
                   + _dot(p_h, jnp.concatenate([vh_scr[s, :, sl], ones], axis=1)))
            l = acc[:, LANES:LANES + 1]
            o_ref[rows, sl] = (acc[:, :LANES] * (1.0 / l)).astype(BF16)
            lse_all = jnp.where(lane == h, (m + jnp.log2(l)) * LN2, lse_all)
        lse_ref[rows, :] = lse_all


def _attention_group(qkv, part0, dil):
    bsz, _, n, _ = qkv.shape
    nsub = min(_ATTN_NSUB, n // _ATTN_QB)
    qb = nsub * _ATTN_QB
    nblk = n // qb
    half = ATTN_HALF
    hb = qb // half
    nhalf = n // half
    w = ATTN_WIDTH
    cur = lambda part: pl.BlockSpec((None, None, qb, w), lambda b, r, j: (b, r, j, part0 + part))
    prev = lambda part: pl.BlockSpec(
        (None, None, half, w), lambda b, r, j: (b, r, jnp.maximum(j * hb - 1, 0), part0 + part))
    nxt = lambda part: pl.BlockSpec(
        (None, None, half, w), lambda b, r, j: (b, r, jnp.minimum((j + 1) * hb, nhalf - 1), part0 + part))
    return pl.pallas_call(
        functools.partial(_attn_kernel, nblk=nblk),
        out_shape=[jax.ShapeDtypeStruct((bsz, dil, n, w), BF16),
                   jax.ShapeDtypeStruct((bsz, dil, n, LANES), F32)],
        grid=(bsz, dil, nblk),
        in_specs=[cur(0), cur(1), cur(2), prev(1), prev(2), nxt(1), nxt(2),
                  _resident((ATTN_HEADS, _ATTN_QB, 2 * _ATTN_QB))],
        out_specs=[pl.BlockSpec((None, None, qb, w), lambda b, r, j: (b, r, j, 0)),
                   pl.BlockSpec((None, None, qb, LANES), lambda b, r, j: (b, r, j, 0))],
        scratch_shapes=[pltpu.VMEM((nsub, 2 * half, w), BF16), pltpu.VMEM((nsub, 2 * half, w), BF16)],
        compiler_params=_params(("parallel", "parallel", "parallel")),
        name=f"band_attention_d{dil}",
    )(qkv, qkv, qkv, qkv, qkv, qkv, qkv, _alibi_table(dil, _ATTN_QB))


def _softplus(x):
    return jnp.maximum(x, 0.0) + jnp.log(1.0 + jnp.exp(-jnp.abs(x)))


def _head_expand(cols, lane):
    out = cols[-1]
    for r in range(SSM_HEADS_PER_GROUP - 2, -1, -1):
        out = jnp.where(lane < (r + 1) * SSM_HEAD_DIM, cols[r], out)
    return out


def _dt_terms(dt_ref, par_ref):
    a_row = -jnp.exp(par_ref[0:1, :])
    dtv = _softplus(dt_ref[...] + par_ref[1:2, :])
    return dtv, dtv * a_row


def _ssd_fwd_kernel(xm_ref, xp_ref, xn_ref, dt_ref, cw_ref, cb_ref, par_ref, dsk_ref,
                    xc_ref, y1_ref, xf_scr, st_scr, *, nc):
    c = pl.program_id(1)
    L = SSD_CHUNK
    H = SSM_HEADS
    R = SSM_HEADS_PER_GROUP
    N = SSM_STATE
    GW = GROUP_WIDTH

    @pl.when(c == 0)
    def _():
        st_scr[...] = jnp.zeros_like(st_scr)

    has_prev = c > 0
    has_next = c < nc - 1
    cwid = 512
    for cc in range(XBC_WIDTH // cwid):
        cols = slice(cc * cwid, (cc + 1) * cwid)
        xm = xm_ref[:, cols].astype(F32)
        xp = jnp.where(has_prev, xp_ref[:, cols].astype(F32), 0.0)
        xn = jnp.where(has_next, xn_ref[:, cols].astype(F32), 0.0)
        xcat = jnp.concatenate([xp, xm, xn], axis=0)
        acc = cb_ref[:, cols] + cw_ref[0:1, cols] * xcat[CONV_HALO - 2:CONV_HALO - 2 + L]
        for k in range(1, SSM_CONV):
            off = CONV_HALO - 2 + k
            acc = acc + cw_ref[k:k + 1, cols] * xcat[off:off + L]
        sv = acc * (1.0 / (1.0 + jnp.exp(-acc)))
        xf_scr[:, cols] = sv
        xc_ref[:, cols] = sv.astype(BF16)

    dtv, dta = _dt_terms(dt_ref, par_ref)
    ti = lax.broadcasted_iota(jnp.int32, (L, L), 0)
    si = lax.broadcasted_iota(jnp.int32, (L, L), 1)
    lower = si <= ti
    strict_lower = si < ti
    strict_upper = si > ti
    a_f = _dot_exact(lower.astype(F32), dta)
    g_b = _dot_exact((si >= ti).astype(F32), dta)
    a_f_t = a_f.T
    g_b_t = g_b.T
    dt_t = dtv.T
    exp_af = jnp.exp(a_f)
    a_end = a_f[L - 1:L, :]
    w_end = jnp.exp(a_end - a_f) * dtv
    cd_row = jnp.exp(a_end)
    lane = lax.broadcasted_iota(jnp.int32, (L, GW), 1)
    lane_n = lax.broadcasted_iota(jnp.int32, (N, GW), 1)

    for g in range(SSM_GROUPS):
        gsl = slice(g * GW, (g + 1) * GW)
        xg = xf_scr[:, gsl]
        xg16 = xg.astype(BF16)
        bg = xf_scr[:, SSM_INNER + g * N:SSM_INNER + (g + 1) * N]
        cg16 = xf_scr[:, SSM_INNER + SSM_GROUPS * N + g * N:
                      SSM_INNER + SSM_GROUPS * N + (g + 1) * N].astype(BF16)
        bg16 = bg.astype(BF16)
        cb = _dot_nt(cg16, bg16)
        ms, xs = [], []
        for r in range(R):
            h = g * R + r
            e = jnp.where(lower, a_f[:, h:h + 1] - a_f_t[h:h + 1, :],
                          g_b[:, H + h:H + h + 1] - g_b_t[H + h:H + h + 1, :])
            dtf_row = dt_t[h:h + 1, :]
            dtb_row = dt_t[H + h:H + h + 1, :]
            dsel = jnp.where(strict_lower, dtf_row, jnp.where(strict_upper, dtb_row, dtf_row + dtb_row))
            ms.append((cb * (jnp.exp(e) * dsel)).astype(BF16))
            xs.append(jnp.where((lane >= r * SSM_HEAD_DIM) & (lane < (r + 1) * SSM_HEAD_DIM),
                                xg16, jnp.zeros_like(xg16)))
        y = _dot(jnp.concatenate(ms, axis=1), jnp.concatenate(xs, axis=0))
        st = st_scr[g]
        y = y + _dot(cg16, st.astype(BF16)) * _head_expand(
            [exp_af[:, g * R + r:g * R + r + 1] for r in range(R)], lane)
        y = y + xg * dsk_ref[:, gsl]
        y1_ref[:, gsl] = y.astype(BF16)
        xw = (xg * _head_expand([w_end[:, g * R + r:g * R + r + 1] for r in range(R)], lane)).astype(BF16)
        s_new = _dot_tn(bg16, xw)
        st_scr[g] = st * _head_expand([cd_row[:, g * R + r:g * R + r + 1] for r in range(R)], lane_n) + s_new


def _ssd_bwd_kernel(xc_ref, dt_ref, par_ref, y1_ref, z_ref, ng_ref, y_ref, st_scr):
    c = pl.program_id(1)
    L = SSD_CHUNK
    H = SSM_HEADS
    R = SSM_HEADS_PER_GROUP
    N = SSM_STATE
    GW = GROUP_WIDTH

    @pl.when(c == 0)
    def _():
        st_scr[...] = jnp.zeros_like(st_scr)

    dtv, dta = _dt_terms(dt_ref, par_ref)
    ti = lax.broadcasted_iota(jnp.int32, (L, L), 0)
    si = lax.broadcasted_iota(jnp.int32, (L, L), 1)
    g_b = _dot_exact((si >= ti).astype(F32), dta)
    exp_gb = jnp.exp(g_b)
    g_start = g_b[0:1, :]
    w_start = jnp.exp(g_start - g_b) * dtv
    cd_row = jnp.exp(g_start)
    lane = lax.broadcasted_iota(jnp.int32, (L, GW), 1)
    lane_n = lax.broadcasted_iota(jnp.int32, (N, GW), 1)

    for g in range(SSM_GROUPS):
        gsl = slice(g * GW, (g + 1) * GW)
        xg = xc_ref[:, gsl].astype(F32)
        bg16 = xc_ref[:, SSM_INNER + g * N:SSM_INNER + (g + 1) * N]
        cg16 = xc_ref[:, SSM_INNER + SSM_GROUPS * N + g * N:SSM_INNER + SSM_GROUPS * N + (g + 1) * N]
        st = st_scr[g]
        heads = [H + g * R + r for r in range(R)]
        y = y1_ref[:, gsl].astype(F32) + _dot(cg16, st.astype(BF16)) * _head_expand(
            [exp_gb[:, h:h + 1] for h in heads], lane)
        zg = z_ref[:, gsl].astype(F32)
        y = y * (zg * (1.0 / (1.0 + jnp.exp(-zg))))
        ms = jnp.mean(y * y, axis=-1, keepdims=True)
        y_ref[:, gsl] = (y * lax.rsqrt(ms + NORM_EPS) * ng_ref[:, gsl]).astype(BF16)
        xw = (xg * _head_expand([w_start[:, h:h + 1] for h in heads], lane)).astype(BF16)
        s_new = _dot_tn(bg16, xw)
        st_scr[g] = st * _head_expand([cd_row[:, h:h + 1] for h in heads], lane_n) + s_new


def _bidirectional_ssd(nat, dt, conv_w, conv_b, ssm_par, d_skip_x, norm_g, bsz, seq):
    L = SSD_CHUNK
    nc = seq // L
    hpc = L // CONV_HALO
    nhalo = seq // CONV_HALO
    nat3 = nat.reshape(bsz, seq, NAT_WIDTH)
    dt3 = dt.reshape(bsz, seq, LANES)
    xbc_blk = NAT_XBC // XBC_WIDTH
    z_blk = NAT_Z // SSM_INNER
    state = pltpu.VMEM((SSM_GROUPS, SSM_STATE, GROUP_WIDTH), F32)
    xc, y1 = pl.pallas_call(
        functools.partial(_ssd_fwd_kernel, nc=nc),
        out_shape=[jax.ShapeDtypeStruct((bsz, seq, XBC_WIDTH), BF16),
                   jax.ShapeDtypeStruct((bsz, seq, SSM_INNER), BF16)],
        grid=(bsz, nc),
        in_specs=[
            pl.BlockSpec((None, L, XBC_WIDTH), lambda b, c: (b, c, xbc_blk)),
            pl.BlockSpec((None, CONV_HALO, XBC_WIDTH), lambda b, c: (b, jnp.maximum(c * hpc - 1, 0), xbc_blk)),
            pl.BlockSpec((None, CONV_HALO, XBC_WIDTH),
                         lambda b, c: (b, jnp.minimum((c + 1) * hpc, nhalo - 1), xbc_blk)),
            pl.BlockSpec((None, L, LANES), lambda b, c: (b, c, 0)),
            _resident((SSM_CONV, XBC_WIDTH)), _resident((1, XBC_WIDTH)), _resident((8, LANES)),
            _resident((1, SSM_INNER)),
        ],
        out_specs=[pl.BlockSpec((None, L, XBC_WIDTH), lambda b, c: (b, c, 0)),
                   pl.BlockSpec((None, L, SSM_INNER), lambda b, c: (b, c, 0))],
        scratch_shapes=[pltpu.VMEM((L, XBC_WIDTH), F32), state],
        compiler_params=_params(("parallel", "arbitrary")),
        name="ssd_forward_sweep",
    )(nat3, nat3, nat3, dt3, conv_w, conv_b, ssm_par, d_skip_x)
    rev = lambda width, blk=0: pl.BlockSpec((None, L, width), lambda b, c: (b, nc - 1 - c, blk))
    y = pl.pallas_call(
        _ssd_bwd_kernel,
        out_shape=jax.ShapeDtypeStruct((bsz, seq, SSM_INNER), BF16),
        grid=(bsz, nc),
        in_specs=[rev(XBC_WIDTH), rev(LANES), _resident((8, LANES)), rev(SSM_INNER),
                  rev(SSM_INNER, z_blk), _resident((1, SSM_INNER))],
        out_specs=rev(SSM_INNER),
        scratch_shapes=[state],
        compiler_params=_params(("parallel", "arbitrary")),
        name="ssd_backward_sweep",
    )(xc, dt3, ssm_par, y1, nat3, norm_g)
    return y.reshape(bsz * seq, SSM_INNER)


_MERGE_TM = 512
_MERGE_NB = 512
_MERGE_PB = 256


def _interleave_matrix(tm, dil):
    p = np.zeros((tm, tm), np.float32)
    t = np.arange(tm)
    p[t, (t % dil) * (tm // dil) + t // dil] = 1.0
    return jnp.asarray(p, BF16)


def _split3(x):
    hi = x.astype(BF16)
    r1 = x - hi.astype(F32)
    mid = r1.astype(BF16)
    lo = (r1 - mid.astype(F32)).astype(BF16)
    return hi, mid, lo


def _merge_kernel(o0_ref, o1_ref, o2_ref, l0_ref, l1_ref, l2_ref, p1_ref, p2_ref, ys_ref, gate_ref, bg_ref,
                  wa_ref, ws_ref, out_ref, ya_scr):
    tm, d = out_ref.shape

    def token_order(o_ref, l_ref, p_ref):
        dil = o_ref.shape[0]
        pb = p_ref.shape[0]
        rpb = pb // dil
        w = o_ref.shape[2]
        o_tok, l_tok = [], []
        for blk in range(tm // pb):
            rs = slice(blk * rpb, (blk + 1) * rpb)
            flat = jnp.concatenate(
                [jnp.concatenate((o_ref[r, rs, :],) + _split3(l_ref[r, rs, :]), axis=1) for r in range(dil)],
                axis=0)
            res = _dot(p_ref[...], flat)
            o_tok.append(res[:, :w])
            l_tok.append(res[:, w:w + LANES] + res[:, w + LANES:w + 2 * LANES] + res[:, w + 2 * LANES:])
        return jnp.concatenate(o_tok, axis=0), jnp.concatenate(l_tok, axis=0)

    o1, l1 = token_order(o1_ref, l1_ref, p1_ref)
    o2, l2 = token_order(o2_ref, l2_ref, p2_ref)
    l0 = l0_ref[...]
    m = jnp.maximum(l0, jnp.maximum(l1, l2))
    e0, e1, e2 = jnp.exp(l0 - m), jnp.exp(l1 - m), jnp.exp(l2 - m)
    inv = 1.0 / (e0 + e1 + e2)
    w0, w1, w2 = e0 * inv, e1 * inv, e2 * inv
    for h in range(ATTN_HEADS):
        sl = slice(h * ATTN_HEAD_DIM, (h + 1) * ATTN_HEAD_DIM)
        ya_scr[:, sl] = (w0[:, h:h + 1] * o0_ref[:, sl].astype(F32) + w1[:, h:h + 1] * o1[:, sl]
                         + w2[:, h:h + 1] * o2[:, sl]).astype(BF16)
    nb = _MERGE_NB
    for cb in range(d // nb):
        cs = slice(cb * nb, (cb + 1) * nb)
        cs2 = slice(d + cb * nb, d + (cb + 1) * nb)
        ga = gate_ref[:, cs].astype(F32) + bg_ref[:, cs]
        gs = gate_ref[:, cs2].astype(F32) + bg_ref[:, cs2]
        a = _dot(ya_scr[...], wa_ref[:, cs])
        s = _dot(ys_ref[...], ws_ref[:, cs])
        out_ref[:, cs] = (a * (1.0 / (1.0 + jnp.exp(-ga))) + s * (1.0 / (1.0 + jnp.exp(-gs)))).astype(BF16)


def _branch_merge(o0, o1, o2, l0, l1, l2, y_ssm, nat, b_gate, w_attn_br, w_ssm_br, bsz):
    t = y_ssm.shape[0]
    d = D_MODEL
    seq = t // bsz
    tm = min(_MERGE_TM, seq // ATTN_GROUPS[2][1])
    tpb = seq // tm
    pb = min(_MERGE_PB, tm)
    row = lambda width, blk=0: pl.BlockSpec((tm, width), lambda i: (i, blk))

    def dil_spec(arr):
        dil, width = arr.shape[1], arr.shape[3]
        return pl.BlockSpec((None, dil, tm // dil, width), lambda i: (i // tpb, 0, i % tpb, 0))

    return pl.pallas_call(
        _merge_kernel,
        out_shape=jax.ShapeDtypeStruct((t, d), BF16),
        grid=(t // tm,),
        in_specs=[row(ATTN_WIDTH), dil_spec(o1), dil_spec(o2), row(LANES), dil_spec(l1), dil_spec(l2),
                  _resident((pb, pb)), _resident((pb, pb)),
                  row(SSM_INNER), row(2 * d, NAT_GATE // (2 * d)), _resident((1, 2 * d)),
                  _resident((ATTN_WIDTH, d)), _resident((SSM_INNER, d))],
        out_specs=row(d),
        scratch_shapes=[pltpu.VMEM((tm, ATTN_WIDTH), BF16)],
        compiler_params=_params(("parallel",)),
        name="branch_merge",
    )(o0, o1, o2, l0, l1, l2, _interleave_matrix(pb, o1.shape[1]), _interleave_matrix(pb, o2.shape[1]),
      y_ssm, nat, b_gate, w_attn_br, w_ssm_br)


def _outproj_kernel(mg_ref, w_ref, x_ref, mod_ref, gain_ref, xo_ref, h2_ref):
    xn = x_ref[...] + mod_ref[2:3, :] * _dot(mg_ref[...], w_ref[...])
    xo_ref[...] = xn
    ms = jnp.mean(xn * xn, axis=-1, keepdims=True)
    y = xn * lax.rsqrt(ms + NORM_EPS) * gain_ref[...]
    h2_ref[...] = (y * (1.0 + mod_ref[4:5, :]) + mod_ref[3:4, :]).astype(BF16)


def _out_projection(merged, w_out, x2, mod_l, gain, bsz):
    t, d = x2.shape
    seq = t // bsz
    tm = min(512, seq)
    tpb = seq // tm
    row = lambda: pl.BlockSpec((tm, d), lambda i: (i, 0))
    return pl.pallas_call(
        _outproj_kernel,
        out_shape=[jax.ShapeDtypeStruct((t, d), F32), jax.ShapeDtypeStruct((t, d), BF16)],
        grid=(t // tm,),
        in_specs=[row(), _resident((d, d)), row(),
                  pl.BlockSpec((None, 6, d), lambda i: (i // tpb, 0, 0)),
                  _resident((1, d))],
        out_specs=[row(), row()],
        compiler_params=_params(("parallel",)),
        name="out_projection",
    )(merged, w_out, x2, mod_l, gain)


_MLP_TM = 1024
_MLP_TK = 512
_MLP_NB = 512
_MLP_ROWS = 256


def _mlp_kernel(h_ref, w1_ref, w2_ref, x_ref, mod_ref, fg_ref, o_ref, *, nk, final_norm):
    k = pl.program_id(1)
    tm, d = o_ref.shape

    @pl.when(k == 0)
    def _():
        o_ref[...] = jnp.zeros_like(o_ref)

    u = jnp.maximum(_dot(h_ref[...], w1_ref[...]), 0.0)
    u2 = (u * u).astype(BF16)
    for cb in range(d // _MLP_NB):
        cs = slice(cb * _MLP_NB, (cb + 1) * _MLP_NB)
        o_ref[:, cs] += _dot(u2, w2_ref[:, cs])

    @pl.when(k == nk - 1)
    def _():
        for r in range(tm // _MLP_ROWS):
            rs = pl.ds(r * _MLP_ROWS, _MLP_ROWS)
            xn = x_ref[rs, :] + mod_ref[5:6, :] * o_ref[rs, :]
            if final_norm:
                ms = jnp.mean(xn * xn, axis=-1, keepdims=True)
                xn = xn * lax.rsqrt(ms + NORM_EPS) * fg_ref[...]
            o_ref[rs, :] = xn


def _mlp(h2, w1, w2, x2, mod_l, final_gain, bsz, final_norm):
    t, d = x2.shape
    seq = t // bsz
    tm = min(_MLP_TM, seq)
    tk = _MLP_TK
    nk = w1.shape[1] // tk
    tpb = seq // tm
    return pl.pallas_call(
        functools.partial(_mlp_kernel, nk=nk, final_norm=final_norm),
        out_shape=jax.ShapeDtypeStruct((t, d), F32),
        grid=(t // tm, nk),
        in_specs=[pl.BlockSpec((tm, d), lambda i, k: (i, 0)),
                  pl.BlockSpec((d, tk), lambda i, k: (0, k)),
                  pl.BlockSpec((tk, d), lambda i, k: (k, 0)),
                  pl.BlockSpec((tm, d), lambda i, k: (i, 0), pipeline_mode=pl.Buffered(1)),
                  pl.BlockSpec((None, 6, d), lambda i, k: (i // tpb, 0, 0)),
                  _resident((1, d))],
        out_specs=pl.BlockSpec((tm, d), lambda i, k: (i, 0)),
        compiler_params=_params(("parallel", "arbitrary")),
        name="relu2_mlp",
    )(h2, w1, w2, x2, mod_l, final_gain)


def _layer(x2, bsz, seq, mod_l, norm_mix, w_in, b_gate, conv_w, conv_b, a_log, dt_bias, d_skip, ssm_norm,
           w_attn_br, w_ssm_br, w_out, norm_mlp, w_mlp_in, w_mlp_out, final_gain, final_norm):
    d = D_MODEL
    g0, g1, g2 = (w_in[:, g * QKV_GROUP_COLS:(g + 1) * QKV_GROUP_COLS] for g in range(3))
    w_main = jnp.concatenate([w_in[:, OFF_XBC:OFF_DT], w_in[:, OFF_GATE:], w_in[:, OFF_Z:OFF_XBC], g0, g1, g2],
                             axis=1).astype(BF16)
    w_dt =jnp.pad(w_in[:, OFF_DT:OFF_GATE], ((0, 0), (0, LANES - 2 * SSM_HEADS)))
    w_dt_hi = w_dt.astype(BF16)
    w_dt_lo = (w_dt - w_dt_hi.astype(F32)).astype(BF16)
    w_dt2 = jnp.concatenate([w_dt_hi, w_dt_lo], axis=1)
    ssm_par = jnp.pad(jnp.stack([a_log.reshape(-1), dt_bias.reshape(-1)]),
                      ((0, 6), (0, LANES - 2 * SSM_HEADS)))
    d_skip_x = jnp.repeat(d_skip, SSM_HEAD_DIM).reshape(1, SSM_INNER)

    nat, q1, q2, dt = _in_projection(x2, mod_l, norm_mix.reshape(1, d), w_main, w_dt2, bsz)
    nat4 = nat.reshape(bsz, 1, seq, NAT_WIDTH)
    o0, l0 = _attention_group(nat4, NAT_QKV // ATTN_WIDTH, 1)
    o1, l1 = _attention_group(q1, 0, ATTN_GROUPS[1][1])
    o2, l2 = _attention_group(q2, 0, ATTN_GROUPS[2][1])
    y_ssm = _bidirectional_ssd(nat, dt, conv_w.reshape(SSM_CONV, XBC_WIDTH), conv_b.reshape(1, XBC_WIDTH),
                               ssm_par, d_skip_x, ssm_norm.reshape(1, SSM_INNER), bsz, seq)
    merged = _branch_merge(o0.reshape(bsz * seq, ATTN_WIDTH), o1, o2, l0.reshape(bsz * seq, LANES), l1, l2,
                           y_ssm, nat, b_gate.reshape(1, 2 * d),
                           w_attn_br.astype(BF16), w_ssm_br.astype(BF16), bsz)
    x2, h2 = _out_projection(merged, w_out.astype(BF16), x2, mod_l, norm_mlp.reshape(1, d), bsz)
    return _mlp(h2, w_mlp_in.astype(BF16), w_mlp_out.astype(BF16), x2, mod_l, final_gain, bsz, final_norm)


def kernel(x, c, w_mod, b_mod, norm_mix, w_in, b_gate, conv_w, conv_b, a_log, dt_bias, d_skip, ssm_norm,
           w_attn_br, w_ssm_br, w_out, norm_mlp, w_mlp_in, w_mlp_out, norm_final):
    bsz, seq, d = x.shape
    depth = w_mod.shape[0]
    mod = _modulation(c, w_mod, b_mod).reshape(depth, bsz, 6, d)
    x2 = x.reshape(bsz * seq, d)
    final_gain = norm_final.reshape(1, d)
    for i in range(depth):
        x2 = _layer(x2, bsz, seq, mod[i], norm_mix[i], w_in[i], b_gate[i], conv_w[i], conv_b[i], a_log[i],
                    dt_bias[i], d_skip[i], ssm_norm[i], w_attn_br[i], w_ssm_br[i], w_out[i], norm_mlp[i],
                    w_mlp_in[i], w_mlp_out[i], final_gain, i == depth - 1)
    return x2.reshape(bsz, seq, d)
```

```python
import functools
import math

import numpy as np
import jax
import jax.numpy as jnp
from jax import lax
from jax.experimental import pallas as pl
from jax.experimental.pallas import tpu as pltpu

F32 = jnp.float32
BF16 = jnp.bfloat16

D_MODEL = 2048
ATTN_GROUPS = ((128, 1), (512, 4), (2048, 16))
ATTN_HEADS = 8
ATTN_HEAD_DIM = 128
ATTN_WIDTH = ATTN_HEADS * ATTN_HEAD_DIM
ATTN_HALF = 64
ALIBI_SLOPES = tuple(2.0 ** (-8.0 * (j + 1) / ATTN_HEADS) for j in range(ATTN_HEADS))
SSM_INNER = D_MODEL
SSM_HEAD_DIM = 64
SSM_HEADS = SSM_INNER // SSM_HEAD_DIM
SSM_GROUPS = 8
SSM_HEADS_PER_GROUP = SSM_HEADS // SSM_GROUPS
SSM_STATE = 128
SSM_CONV = 5
XBC_WIDTH = SSM_INNER + 2 * SSM_GROUPS * SSM_STATE
GROUP_WIDTH = SSM_HEADS_PER_GROUP * SSM_HEAD_DIM
MLP_HIDDEN = 4 * D_MODEL
NORM_EPS = 1e-6
QKV_GROUP_COLS = 3 * ATTN_WIDTH
QKV_COLS = len(ATTN_GROUPS) * QKV_GROUP_COLS
OFF_Z = QKV_COLS
OFF_XBC = OFF_Z + SSM_INNER
OFF_DT = OFF_XBC + XBC_WIDTH
OFF_GATE = OFF_DT + 2 * SSM_HEADS
IN_WIDTH = OFF_GATE + 2 * D_MODEL

LANES = 128
SSD_CHUNK = 128
CONV_HALO = 16
NEG_BIG = -1e30
LOG2E = math.log2(math.e)
LN2 = math.log(2.0)
VMEM_LIMIT = 56 * 1024 * 1024

NAT_XBC = 0
NAT_GATE = NAT_XBC + XBC_WIDTH
NAT_Z = NAT_GATE + 2 * D_MODEL
NAT_QKV = NAT_Z + SSM_INNER
NAT_WIDTH = NAT_QKV + QKV_GROUP_COLS


def _params(sem, vmem=VMEM_LIMIT):
    return pltpu.CompilerParams(dimension_semantics=sem, vmem_limit_bytes=vmem)


def _dot(a, b):
    return jnp.dot(a, b, preferred_element_type=F32)


def _dot_nt(a, b):
    return lax.dot_general(a, b, (((1,), (1,)), ((), ())), preferred_element_type=F32)


def _dot_tn(a, b):
    return lax.dot_general(a, b, (((0,), (0,)), ((), ())), preferred_element_type=F32)


def _dot_exact(a, b):
    return jnp.dot(a, b, preferred_element_type=F32, precision=lax.Precision.HIGHEST)


def _resident(shape):
    return pl.BlockSpec(shape, lambda *_: (0,) * len(shape), pipeline_mode=pl.Buffered(1))


def _resident_layer(shape, li):
    return pl.BlockSpec((None,) + tuple(shape), lambda *_: (li,) + (0,) * len(shape),
                        pipeline_mode=pl.Buffered(1))


def _mod_kernel(ct_ref, w_ref, b_ref, o_ref):
    bsz = o_ref.shape[0]
    w = w_ref[...]
    rows = [jnp.sum(w * ct_ref[:, b:b + 1], axis=0, keepdims=True) for b in range(bsz)]
    o_ref[...] = jnp.concatenate(rows, axis=0) + b_ref[...]


def _modulation(c, w_mod, b_mod):
    depth, d, n = w_mod.shape
    bsz = c.shape[0]
    tn = 2048
    return pl.pallas_call(
        _mod_kernel,
        out_shape=jax.ShapeDtypeStruct((depth, bsz, n), F32),
        grid=(depth, n // tn),
        in_specs=[
            pl.BlockSpec((d, bsz), lambda l, j: (0, 0)),
            pl.BlockSpec((None, d, tn), lambda l, j: (l, 0, j)),
            pl.BlockSpec((None, 1, tn), lambda l, j: (l, 0, j)),
        ],
        out_specs=pl.BlockSpec((None, bsz, tn), lambda l, j: (l, 0, j)),
        compiler_params=_params(("parallel", "parallel")),
        name="adaln_mod",
    )(c.T, w_mod, b_mod.reshape(depth, 1, n))


_IN_TM = 1024
_IN_TN = 1024
_NAT_TILES = NAT_WIDTH // _IN_TN
_QKV_TILES = QKV_GROUP_COLS // _IN_TN
_IN_TILES = _NAT_TILES + 2 * _QKV_TILES
_NORM_ROWS = 256


def _modulated_norm(x_ref, gain_ref, shift, scale, r0, rows):
    xf = x_ref[pl.ds(r0, rows), :]
    ms = jnp.mean(xf * xf, axis=-1, keepdims=True)
    y = xf * lax.rsqrt(ms + NORM_EPS) * gain_ref[...]
    return y * (1.0 + scale) + shift


def _inproj_kernel(x_ref, mod_ref, gain_ref, w_ref, wdt_ref, nat_ref, q1_ref, q2_ref, dt_ref, h_scr, r_scr):
    j = pl.program_id(1)
    tm = x_ref.shape[0]

    @pl.when(j == 0)
    def _():
        shift = mod_ref[0:1, :]
        scale = mod_ref[1:2, :]
        for r in range(tm // _NORM_ROWS):
            r0 = r * _NORM_ROWS
            h = _modulated_norm(x_ref, gain_ref, shift, scale, r0, _NORM_ROWS)
            h_hi = h.astype(BF16)
            h_lo = (h - h_hi.astype(F32)).astype(BF16)
            h_scr[pl.ds(r0, _NORM_ROWS), :] = h_hi
            both = _dot(h_hi, wdt_ref[...])
            dt_ref[pl.ds(r0, _NORM_ROWS), :] = (both[:, :LANES] + both[:, LANES:]
                                                + _dot(h_lo, wdt_ref[:, :LANES]))

    @pl.when(j < _NAT_TILES)
    def _():
        nat_ref[...] = _dot(h_scr[...], w_ref[...]).astype(BF16)

    for lo, ref in ((_NAT_TILES, q1_ref), (_NAT_TILES + _QKV_TILES, q2_ref)):
        @pl.when((j >= lo) & (j < lo + _QKV_TILES))
        def _(ref=ref):
            dil, rows = ref.shape[0], ref.shape[1]
            res = _dot(h_scr[...], w_ref[...])
            for cblk in range(_IN_TN // LANES):
                r_scr[cblk] = res[:, cblk * LANES:(cblk + 1) * LANES]
            for r in range(dil):
                for cblk in range(_IN_TN // LANES):
                    ref[r, :, cblk * LANES:(cblk + 1) * LANES] = (
                        r_scr[cblk, pl.ds(r, rows, stride=dil), :].astype(BF16))


def _in_projection(x2, mod_l, gain, w_main, w_dt, li, bsz):
    t, d = x2.shape
    seq = t // bsz
    tm = min(_IN_TM, seq)
    tpb = seq // tm
    tn = _IN_TN
    d1, d2 = ATTN_GROUPS[1][1], ATTN_GROUPS[2][1]

    def dil_spec(dil, lo):
        return pl.BlockSpec((None, dil, tm // dil, tn),
                            lambda i, j: (i // tpb, 0, i % tpb, jnp.clip(j - lo, 0, _QKV_TILES - 1)))

    return pl.pallas_call(
        _inproj_kernel,
        out_shape=[jax.ShapeDtypeStruct((t, NAT_WIDTH), BF16),
                   jax.ShapeDtypeStruct((bsz, d1, seq // d1, QKV_GROUP_COLS), BF16),
                   jax.ShapeDtypeStruct((bsz, d2, seq // d2, QKV_GROUP_COLS), BF16),
                   jax.ShapeDtypeStruct((t, LANES), F32)],
        grid=(t // tm, _IN_TILES),
        in_specs=[
            pl.BlockSpec((tm, d), lambda i, j: (i, 0)),
            pl.BlockSpec((None, 6, d), lambda i, j: (i // tpb, 0, 0)),
            _resident((1, d)),
            pl.BlockSpec((None, d, tn), lambda i, j: (li, 0, j)),
            _resident_layer((d, 2 * LANES), li),
        ],
        out_specs=[pl.BlockSpec((tm, tn), lambda i, j: (i, jnp.minimum(j, _NAT_TILES - 1))),
                   dil_spec(d1, _NAT_TILES),
                   dil_spec(d2, _NAT_TILES + _QKV_TILES),
                   pl.BlockSpec((tm, LANES), lambda i, j: (i, 0))],
        scratch_shapes=[pltpu.VMEM((tm, d), BF16), pltpu.VMEM((tn // LANES, tm, LANES), F32)],
        compiler_params=_params(("parallel", "arbitrary")),
        name="in_projection",
    )(x2, mod_l, gain, w_main, w_dt)


_ATTN_QB = 128
_ATTN_NSUB = 4


def _alibi_table(dil, qb):
    half = ATTN_HALF
    row = np.arange(qb)[:, None]
    col = np.arange(qb)[None, :]
    dist_c = np.abs(col - row)
    hcol = np.arange(2 * half)[None, :]
    dist_p = row + half - hcol
    dist_n = (hcol - half) + qb - row
    dist_h = np.where(hcol < half, dist_p, dist_n)
    dist = np.concatenate([dist_c, dist_h], axis=1).astype(np.float64)
    ok = dist <= half
    slopes = np.asarray(ALIBI_SLOPES, np.float64)[:, None, None]
    tbl = np.where(ok[None], -slopes * dist[None] * dil * LOG2E, NEG_BIG)
    return jnp.asarray(tbl, F32)


def _attn_kernel(q_ref, kc_ref, vc_ref, kp_ref, vp_ref, kn_ref, vn_ref, bias_ref, o_ref, lse_ref,
                 kh_scr, vh_scr, *, nblk):
    j = pl.program_id(2)
    qb = _ATTN_QB
    nsub = q_ref.shape[0] // qb
    half = ATTN_HALF
    c1 = ATTN_HEAD_DIM ** -0.5 * LOG2E
    hcol = lax.broadcasted_iota(jnp.int32, (qb, 2 * half), 1)
    lane = lax.broadcasted_iota(jnp.int32, (qb, LANES), 1)
    ones = jnp.ones((qb, LANES), BF16)
    for s in range(nsub):
        rows = pl.ds(s * qb, qb)
        edge_ok = None
        if s == 0:
            kh_scr[s, 0:half, :] = kp_ref[...]
            vh_scr[s, 0:half, :] = vp_ref[...]
            edge_ok = hcol >= jnp.where(j > 0, 0, half)
        else:
            kh_scr[s, 0:half, :] = kc_ref[pl.ds(s * qb - half, half), :]
            vh_scr[s, 0:half, :] = vc_ref[pl.ds(s * qb - half, half), :]
        if s == nsub - 1:
            kh_scr[s, half:2 * half, :] = kn_ref[...]
            vh_scr[s, half:2 * half, :] = vn_ref[...]
            ok_n = hcol < jnp.where(j < nblk - 1, 2 * half, half)
            edge_ok = ok_n if edge_ok is None else edge_ok & ok_n
        else:
            kh_scr[s, half:2 * half, :] = kc_ref[pl.ds((s + 1) * qb, half), :]
            vh_scr[s, half:2 * half, :] = vc_ref[pl.ds((s + 1) * qb, half), :]
        lse_all = jnp.zeros((qb, LANES), F32)
        for h in range(ATTN_HEADS):
            sl = slice(h * ATTN_HEAD_DIM, (h + 1) * ATTN_HEAD_DIM)
            q = q_ref[rows, sl]
            t_c = _dot_nt(q, kc_ref[rows, sl]) * c1 + bias_ref[h, :, 0:qb]
            t_h = _dot_nt(q, kh_scr[s, :, sl]) * c1 + bias_ref[h, :, qb:]
            if edge_ok is not None:
                t_h = jnp.where(edge_ok, t_h, NEG_BIG)
            m = jnp.max(jnp.maximum(t_c, t_h), axis=-1, keepdims=True)
            p_c = jnp.exp2(t_c - m).astype(BF16)
            p_h = jnp.exp2(t_h - m).astype(BF16)
            acc = (_dot(p_c, jnp.concatenate([vc_ref[rows, sl], ones], axis=1))
                   + _dot(p_h, jnp.concatenate([vh_scr[s, :, sl], ones], axis=1)))
            l = acc[:, LANES:LANES + 1]
            o_ref[rows, sl] = (acc[:, :LANES] * (1.0 / l)).astype(BF16)
            lse_all = jnp.where(lane == h, (m + jnp.log2(l)) * LN2, lse_all)
        lse_ref[rows, :] = lse_all


def _attention_group(qkv, part0, dil):
    bsz, _, n, _ = qkv.shape
    nsub = min(_ATTN_NSUB, n // _ATTN_QB)
    qb = nsub * _ATTN_QB
    nblk = n // qb
    half = ATTN_HALF
    hb = qb // half
    nhalf = n // half
    w = ATTN_WIDTH
    cur = lambda part: pl.BlockSpec((None, None, qb, w), lambda b, r, j: (b, r, j, part0 + part))
    prev = lambda part: pl.BlockSpec(
        (None, None, half, w), lambda b, r, j: (b, r, jnp.maximum(j * hb - 1, 0), part0 + part))
    nxt = lambda part: pl.BlockSpec(
        (None, None, half, w), lambda b, r, j: (b, r, jnp.minimum((j + 1) * hb, nhalf - 1), part0 + part))
    return pl.pallas_call(
        functools.partial(_attn_kernel, nblk=nblk),
        out_shape=[jax.ShapeDtypeStruct((bsz, dil, n, w), BF16),
                   jax.ShapeDtypeStruct((bsz, dil, n, LANES), F32)],
        grid=(bsz, dil, nblk),
        in_specs=[cur(0), cur(1), cur(2), prev(1), prev(2), nxt(1), nxt(2),
                  _resident((ATTN_HEADS, _ATTN_QB, 2 * _ATTN_QB))],
        out_specs=[pl.BlockSpec((None, None, qb, w), lambda b, r, j: (b, r, j, 0)),
                   pl.BlockSpec((None, None, qb, LANES), lambda b, r, j: (b, r, j, 0))],
        scratch_shapes=[pltpu.VMEM((nsub, 2 * half, w), BF16), pltpu.VMEM((nsub, 2 * half, w), BF16)],
        compiler_params=_params(("parallel", "parallel", "parallel")),
        name=f"band_attention_d{dil}",
    )(qkv, qkv, qkv, qkv, qkv, qkv, qkv, _alibi_table(dil, _ATTN_QB))


def _softplus(x):
    return jnp.maximum(x, 0.0) + jnp.log(1.0 + jnp.exp(-jnp.abs(x)))


def _head_expansion_matrix(first_lane):
    e = np.zeros((LANES, SSM_INNER), np.float32)
    for h in range(SSM_HEADS):
        e[first_lane + h, h * SSM_HEAD_DIM:(h + 1) * SSM_HEAD_DIM] = 1.0
    return jnp.asarray(np.concatenate([e, e, e], axis=0), BF16)


def _expand_heads(q, e3_ref):
    return _dot(jnp.concatenate(_split3(q), axis=1), e3_ref[...])


def _dt_terms(dt_ref, par_ref):
    a_row = -jnp.exp(par_ref[0:1, :])
    dtv = _softplus(dt_ref[...] + par_ref[1:2, :])
    return dtv, dtv * a_row


def _ssd_fwd_kernel(xm_ref, xp_ref, xn_ref, dt_ref, cw_ref, cb_ref, par_ref, dsk_ref, e3_ref,
                    xc_ref, y1_ref, xf_scr, st_scr, *, nc):
    c = pl.program_id(1)
    L = SSD_CHUNK
    H = SSM_HEADS
    R = SSM_HEADS_PER_GROUP
    N = SSM_STATE
    GW = GROUP_WIDTH

    @pl.when(c == 0)
    def _():
        st_scr[...] = jnp.zeros_like(st_scr)

    has_prev = c > 0
    has_next = c < nc - 1
    cwid = 512
    for cc in range(XBC_WIDTH // cwid):
        cols = slice(cc * cwid, (cc + 1) * cwid)
        xm = xm_ref[:, cols].astype(F32)
        xp = jnp.where(has_prev, xp_ref[:, cols].astype(F32), 0.0)
        xn = jnp.where(has_next, xn_ref[:, cols].astype(F32), 0.0)
        xcat = jnp.concatenate([xp, xm, xn], axis=0)
        rows_cat = L + 2 * CONV_HALO
        acc = cb_ref[:, cols] + cw_ref[SSM_CONV // 2:SSM_CONV // 2 + 1, cols] * xm
        for k in range(SSM_CONV):
            if k != SSM_CONV // 2:
                shifted = pltpu.roll(xcat, (SSM_CONV // 2 - k) % rows_cat, axis=0)[CONV_HALO:CONV_HALO + L]
                acc = acc + cw_ref[k:k + 1, cols] * shifted
        sv = acc * (1.0 / (1.0 + jnp.exp(-acc)))
        xf_scr[:, cols] = sv
        xc_ref[:, cols] = sv.astype(BF16)

    dtv, dta = _dt_terms(dt_ref, par_ref)
    ti = lax.broadcasted_iota(jnp.int32, (L, L), 0)
    si = lax.broadcasted_iota(jnp.int32, (L, L), 1)
    lower = si <= ti
    strict_lower = si < ti
    strict_upper = si > ti
    a_f = _dot_exact(lower.astype(F32), dta)
    g_b = _dot_exact((si >= ti).astype(F32), dta)
    a_f_t = a_f.T
    g_b_t = g_b.T
    dt_t = dtv.T
    exp_af = jnp.exp(a_f)
    a_end = a_f[L - 1:L, :]
    w_end = jnp.exp(a_end - a_f) * dtv
    cd_row = jnp.exp(a_end)
    lane = lax.broadcasted_iota(jnp.int32, (L, GW), 1)
    fac = _expand_heads(jnp.concatenate([exp_af, w_end, jnp.broadcast_to(cd_row, (8, LANES))], axis=0), e3_ref)

    for g in range(SSM_GROUPS):
        gsl = slice(g * GW, (g + 1) * GW)
        xg = xf_scr[:, gsl]
        xg16 = xg.astype(BF16)
        bg = xf_scr[:, SSM_INNER + g * N:SSM_INNER + (g + 1) * N]
        cg16 = xf_scr[:, SSM_INNER + SSM_GROUPS * N + g * N:
                      SSM_INNER + SSM_GROUPS * N + (g + 1) * N].astype(BF16)
        bg16 = bg.astype(BF16)
        cb = _dot_nt(cg16, bg16)
        ms, xs = [], []
        for r in range(R):
            h = g * R + r
            e = jnp.where(lower, a_f[:, h:h + 1] - a_f_t[h:h + 1, :],
                          g_b[:, H + h:H + h + 1] - g_b_t[H + h:H + h + 1, :])
            dtf_row = dt_t[h:h + 1, :]
            dtb_row = dt_t[H + h:H + h + 1, :]
            dsel = jnp.where(strict_lower, dtf_row, jnp.where(strict_upper, dtb_row, dtf_row + dtb_row))
            ms.append((cb * (jnp.exp(e) * dsel)).astype(BF16))
            xs.append(jnp.where((lane >= r * SSM_HEAD_DIM) & (lane < (r + 1) * SSM_HEAD_DIM),
                                xg16, jnp.zeros_like(xg16)))
        y = _dot(jnp.concatenate(ms, axis=1), jnp.concatenate(xs, axis=0))
        st = st_scr[g]
        y = y + _dot(cg16, st.astype(BF16)) * fac[0:L, gsl]
        y = y + xg * dsk_ref[:, gsl]
        y1_ref[:, gsl] = y.astype(BF16)
        xw = (xg * fac[L:2 * L, gsl]).astype(BF16)
        s_new = _dot_tn(bg16, xw)
        st_scr[g] = st * fac[2 * L:2 * L + 1, gsl] + s_new


def _ssd_bwd_kernel(xc_ref, dt_ref, par_ref, y1_ref, z_ref, ng_ref, e3_ref, y_ref, st_scr):
    c = pl.program_id(1)
    L = SSD_CHUNK
    H = SSM_HEADS
    R = SSM_HEADS_PER_GROUP
    N = SSM_STATE
    GW = GROUP_WIDTH

    @pl.when(c == 0)
    def _():
        st_scr[...] = jnp.zeros_like(st_scr)

    dtv, dta = _dt_terms(dt_ref, par_ref)
    ti = lax.broadcasted_iota(jnp.int32, (L, L), 0)
    si = lax.broadcasted_iota(jnp.int32, (L, L), 1)
    g_b = _dot_exact((si >= ti).astype(F32), dta)
    exp_gb = jnp.exp(g_b)
    g_start = g_b[0:1, :]
    w_start = jnp.exp(g_start - g_b) * dtv
    cd_row = jnp.exp(g_start)
    fac = _expand_heads(jnp.concatenate([exp_gb, w_start, jnp.broadcast_to(cd_row, (8, LANES))], axis=0), e3_ref)

    for g in range(SSM_GROUPS):
        gsl = slice(g * GW, (g + 1) * GW)
        xg = xc_ref[:, gsl].astype(F32)
        bg16 = xc_ref[:, SSM_INNER + g * N:SSM_INNER + (g + 1) * N]
        cg16 = xc_ref[:, SSM_INNER + SSM_GROUPS * N + g * N:SSM_INNER + SSM_GROUPS * N + (g + 1) * N]
        st = st_scr[g]
        y = y1_ref[:, gsl].astype(F32) + _dot(cg16, st.astype(BF16)) * fac[0:L, gsl]
        zg = z_ref[:, gsl].astype(F32)
        y = y * (zg * (1.0 / (1.0 + jnp.exp(-zg))))
        ms = jnp.mean(y * y, axis=-1, keepdims=True)
        y_ref[:, gsl] = (y * lax.rsqrt(ms + NORM_EPS) * ng_ref[:, gsl]).astype(BF16)
        xw = (xg * fac[L:2 * L, gsl]).astype(BF16)
        s_new = _dot_tn(bg16, xw)
        st_scr[g] = st * fac[2 * L:2 * L + 1, gsl] + s_new


def _bidirectional_ssd(nat, dt, conv_w, conv_b, ssm_par, d_skip_x, norm_g, bsz, seq):
    L = SSD_CHUNK
    nc = seq // L
    hpc = L // CONV_HALO
    nhalo = seq // CONV_HALO
    nat3 = nat.reshape(bsz, seq, NAT_WIDTH)
    dt3 = dt.reshape(bsz, seq, LANES)
    xbc_blk = NAT_XBC // XBC_WIDTH
    z_blk = NAT_Z // SSM_INNER
    state = pltpu.VMEM((SSM_GROUPS, SSM_STATE, GROUP_WIDTH), F32)
    xc, y1 = pl.pallas_call(
        functools.partial(_ssd_fwd_kernel, nc=nc),
        out_shape=[jax.ShapeDtypeStruct((bsz, seq, XBC_WIDTH), BF16),
                   jax.ShapeDtypeStruct((bsz, seq, SSM_INNER), BF16)],
        grid=(bsz, nc),
        in_specs=[
            pl.BlockSpec((None, L, XBC_WIDTH), lambda b, c: (b, c, xbc_blk)),
            pl.BlockSpec((None, CONV_HALO, XBC_WIDTH), lambda b, c: (b, jnp.maximum(c * hpc - 1, 0), xbc_blk)),
            pl.BlockSpec((None, CONV_HALO, XBC_WIDTH),
                         lambda b, c: (b, jnp.minimum((c + 1) * hpc, nhalo - 1), xbc_blk)),
            pl.BlockSpec((None, L, LANES), lambda b, c: (b, c, 0)),
            _resident((SSM_CONV, XBC_WIDTH)), _resident((1, XBC_WIDTH)), _resident((8, LANES)),
            _resident((1, SSM_INNER)), _resident((3 * LANES, SSM_INNER)),
        ],
        out_specs=[pl.BlockSpec((None, L, XBC_WIDTH), lambda b, c: (b, c, 0)),
                   pl.BlockSpec((None, L, SSM_INNER), lambda b, c: (b, c, 0))],
        scratch_shapes=[pltpu.VMEM((L, XBC_WIDTH), F32), state],
        compiler_params=_params(("parallel", "arbitrary")),
        name="ssd_forward_sweep",
    )(nat3, nat3, nat3, dt3, conv_w, conv_b, ssm_par, d_skip_x, _head_expansion_matrix(0))
    rev = lambda width, blk=0: pl.BlockSpec((None, L, width), lambda b, c: (b, nc - 1 - c, blk))
    y = pl.pallas_call(
        _ssd_bwd_kernel,
        out_shape=jax.ShapeDtypeStruct((bsz, seq, SSM_INNER), BF16),
        grid=(bsz, nc),
        in_specs=[rev(XBC_WIDTH), rev(LANES), _resident((8, LANES)), rev(SSM_INNER),
                  rev(SSM_INNER, z_blk), _resident((1, SSM_INNER)), _resident((3 * LANES, SSM_INNER))],
        out_specs=rev(SSM_INNER),
        scratch_shapes=[state],
        compiler_params=_params(("parallel", "arbitrary")),
        name="ssd_backward_sweep",
    )(xc, dt3, ssm_par, y1, nat3, norm_g, _head_expansion_matrix(SSM_HEADS))
    return y.reshape(bsz * seq, SSM_INNER)


_MERGE_TM = 512
_MERGE_NB = 512
_MERGE_PB = 256


def _interleave_matrix(tm, dil):
    p = np.zeros((tm, tm), np.float32)
    t = np.arange(tm)
    p[t, (t % dil) * (tm // dil) + t // dil] = 1.0
    return jnp.asarray(p, BF16)


def _split3(x):
    hi = x.astype(BF16)
    r1 = x - hi.astype(F32)
    mid = r1.astype(BF16)
    lo = (r1 - mid.astype(F32)).astype(BF16)
    return hi, mid, lo


def _merge_kernel(o0_ref, o1_ref, o2_ref, l0_ref, l1_ref, l2_ref, p1_ref, p2_ref, ys_ref, gate_ref, bg_ref,
                  wa_ref, ws_ref, out_ref, ya_scr, s_scr):
    tm, d = out_ref.shape

    def token_order(o_ref, l_ref, p_ref):
        dil = o_ref.shape[0]
        pb = p_ref.shape[0]
        rpb = pb // dil
        w = o_ref.shape[2]
        o_tok, l_tok = [], []
        for blk in range(tm // pb):
            rs = slice(blk * rpb, (blk + 1) * rpb)
            flat = jnp.concatenate(
                [jnp.concatenate((o_ref[r, rs, :],) + _split3(l_ref[r, rs, :]), axis=1) for r in range(dil)],
                axis=0)
            res = _dot(p_ref[...], flat)
            o_tok.append(res[:, :w])
            l_tok.append(res[:, w:w + LANES] + res[:, w + LANES:w + 2 * LANES] + res[:, w + 2 * LANES:])
        return jnp.concatenate(o_tok, axis=0), jnp.concatenate(l_tok, axis=0)

    o1, l1 = token_order(o1_ref, l1_ref, p1_ref)
    o2, l2 = token_order(o2_ref, l2_ref, p2_ref)
    nb = _MERGE_NB
    for cb in range(d // nb):
        cs = slice(cb * nb, (cb + 1) * nb)
        s_scr[:, cs] = _dot(ys_ref[...], ws_ref[:, cs])
    l0 = l0_ref[...]
    m = jnp.maximum(l0, jnp.maximum(l1, l2))
    e0, e1, e2 = jnp.exp(l0 - m), jnp.exp(l1 - m), jnp.exp(l2 - m)
    inv = 1.0 / (e0 + e1 + e2)
    w0, w1, w2 = e0 * inv, e1 * inv, e2 * inv
    for h in range(ATTN_HEADS):
        sl = slice(h * ATTN_HEAD_DIM, (h + 1) * ATTN_HEAD_DIM)
        ya_scr[:, sl] = (w0[:, h:h + 1] * o0_ref[:, sl].astype(F32) + w1[:, h:h + 1] * o1[:, sl]
                         + w2[:, h:h + 1] * o2[:, sl]).astype(BF16)
    for cb in range(d // nb):
        cs = slice(cb * nb, (cb + 1) * nb)
        cs2 = slice(d + cb * nb, d + (cb + 1) * nb)
        ga = gate_ref[:, cs].astype(F32) + bg_ref[:, cs]
        gs = gate_ref[:, cs2].astype(F32) + bg_ref[:, cs2]
        a = _dot(ya_scr[...], wa_ref[:, cs])
        s = s_scr[:, cs]
        out_ref[:, cs] = (a * (1.0 / (1.0 + jnp.exp(-ga))) + s * (1.0 / (1.0 + jnp.exp(-gs)))).astype(BF16)


def _branch_merge(o0, o1, o2, l0, l1, l2, y_ssm, nat, b_gate, w_attn_br, w_ssm_br, li, bsz):
    t = y_ssm.shape[0]
    d = D_MODEL
    seq = t // bsz
    tm = min(_MERGE_TM, seq // ATTN_GROUPS[2][1])
    tpb = seq // tm
    pb = min(_MERGE_PB, tm)
    row = lambda width, blk=0: pl.BlockSpec((tm, width), lambda i: (i, blk))

    def dil_spec(arr):
        dil, width = arr.shape[1], arr.shape[3]
        return pl.BlockSpec((None, dil, tm // dil, width), lambda i: (i // tpb, 0, i % tpb, 0))

    return pl.pallas_call(
        _merge_kernel,
        out_shape=jax.ShapeDtypeStruct((t, d), BF16),
        grid=(t // tm,),
        in_specs=[row(ATTN_WIDTH), dil_spec(o1), dil_spec(o2), row(LANES), dil_spec(l1), dil_spec(l2),
                  _resident((pb, pb)), _resident((pb, pb)),
                  row(SSM_INNER), row(2 * d, NAT_GATE // (2 * d)), _resident((1, 2 * d)),
                  _resident_layer((ATTN_WIDTH, d), li), _resident_layer((SSM_INNER, d), li)],
        out_specs=row(d),
        scratch_shapes=[pltpu.VMEM((tm, ATTN_WIDTH), BF16), pltpu.VMEM((tm, d), F32)],
        compiler_params=_params(("parallel",)),
        name="branch_merge",
    )(o0, o1, o2, l0, l1, l2, _interleave_matrix(pb, o1.shape[1]), _interleave_matrix(pb, o2.shape[1]),
      y_ssm, nat, b_gate, w_attn_br, w_ssm_br)


def _outproj_kernel(mg_ref, w_ref, x_ref, mod_ref, gain_ref, xo_ref, h2_ref):
    xn = x_ref[...] + mod_ref[2:3, :] * _dot(mg_ref[...], w_ref[...])
    xo_ref[...] = xn
    ms = jnp.mean(xn * xn, axis=-1, keepdims=True)
    y = xn * lax.rsqrt(ms + NORM_EPS) * gain_ref[...]
    h2_ref[...] = (y * (1.0 + mod_ref[4:5, :]) + mod_ref[3:4, :]).astype(BF16)


def _out_projection(merged, w_out, x2, mod_l, gain, li, bsz):
    t, d = x2.shape
    seq = t // bsz
    tm = min(512, seq)
    tpb = seq // tm
    row = lambda: pl.BlockSpec((tm, d), lambda i: (i, 0))
    return pl.pallas_call(
        _outproj_kernel,
        out_shape=[jax.ShapeDtypeStruct((t, d), F32), jax.ShapeDtypeStruct((t, d), BF16)],
        grid=(t // tm,),
        in_specs=[row(), _resident_layer((d, d), li), row(),
                  pl.BlockSpec((None, 6, d), lambda i: (i // tpb, 0, 0)),
                  _resident((1, d))],
        out_specs=[row(), row()],
        compiler_params=_params(("parallel",)),
        name="out_projection",
    )(merged, w_out, x2, mod_l, gain)


_MLP_TM = 1024
_MLP_TK = 512
_MLP_NB = 512
_MLP_ROWS = 256


def _mlp_kernel(h_ref, w1_ref, w2_ref, x_ref, mod_ref, fg_ref, o_ref, *, nk, final_norm):
    k = pl.program_id(1)
    tm, d = o_ref.shape

    @pl.when(k == 0)
    def _():
        o_ref[...] = jnp.zeros_like(o_ref)

    u = jnp.maximum(_dot(h_ref[...], w1_ref[...]), 0.0)
    u2 = (u * u).astype(BF16)
    for cb in range(d // _MLP_NB):
        cs = slice(cb * _MLP_NB, (cb + 1) * _MLP_NB)
        o_ref[:, cs] += _dot(u2, w2_ref[:, cs])

    @pl.when(k == nk - 1)
    def _():
        for r in range(tm // _MLP_ROWS):
            rs = pl.ds(r * _MLP_ROWS, _MLP_ROWS)
            xn = x_ref[rs, :] + mod_ref[5:6, :] * o_ref[rs, :]
            if final_norm:
                ms = jnp.mean(xn * xn, axis=-1, keepdims=True)
                xn = xn * lax.rsqrt(ms + NORM_EPS) * fg_ref[...]
            o_ref[rs, :] = xn


def _mlp(h2, w1, w2, x2, mod_l, final_gain, li, bsz, final_norm):
    t, d = x2.shape
    seq = t // bsz
    tm = min(_MLP_TM, seq)
    tk = _MLP_TK
    nk = w1.shape[2] // tk
    tpb = seq // tm
    return pl.pallas_call(
        functools.partial(_mlp_kernel, nk=nk, final_norm=final_norm),
        out_shape=jax.ShapeDtypeStruct((t, d), F32),
        grid=(t // tm, nk),
        in_specs=[pl.BlockSpec((tm, d), lambda i, k: (i, 0)),
                  pl.BlockSpec((None, d, tk), lambda i, k: (li, 0, k)),
                  pl.BlockSpec((None, tk, d), lambda i, k: (li, k, 0)),
                  pl.BlockSpec((tm, d), lambda i, k: (i, 0), pipeline_mode=pl.Buffered(1)),
                  pl.BlockSpec((None, 6, d), lambda i, k: (i // tpb, 0, 0)),
                  _resident((1, d))],
        out_specs=pl.BlockSpec((tm, d), lambda i, k: (i, 0)),
        compiler_params=_params(("parallel", "arbitrary")),
        name="relu2_mlp",
    )(h2, w1, w2, x2, mod_l, final_gain)


def _layer(x2, bsz, seq, li, mod_l, norm_mix, w_main, w_dt2, b_gate, conv_w, conv_b, a_log, dt_bias, d_skip,
           ssm_norm, w_attn_br, w_ssm_br, w_out, norm_mlp, w_mlp_in, w_mlp_out, final_gain, final_norm):
    d = D_MODEL
    ssm_par = jnp.pad(jnp.stack([a_log.reshape(-1), dt_bias.reshape(-1)]),
                      ((0, 6), (0, LANES - 2 * SSM_HEADS)))
    d_skip_x = jnp.repeat(d_skip, SSM_HEAD_DIM).reshape(1, SSM_INNER)

    nat, q1, q2, dt = _in_projection(x2, mod_l, norm_mix.reshape(1, d), w_main, w_dt2, li, bsz)
    nat4 = nat.reshape(bsz, 1, seq, NAT_WIDTH)
    o0, l0 = _attention_group(nat4, NAT_QKV // ATTN_WIDTH, 1)
    o1, l1 = _attention_group(q1, 0, ATTN_GROUPS[1][1])
    o2, l2 = _attention_group(q2, 0, ATTN_GROUPS[2][1])
    y_ssm = _bidirectional_ssd(nat, dt, conv_w.reshape(SSM_CONV, XBC_WIDTH), conv_b.reshape(1, XBC_WIDTH),
                               ssm_par, d_skip_x, ssm_norm.reshape(1, SSM_INNER), bsz, seq)
    merged = _branch_merge(o0.reshape(bsz * seq, ATTN_WIDTH), o1, o2, l0.reshape(bsz * seq, LANES), l1, l2,
                           y_ssm, nat, b_gate.reshape(1, 2 * d), w_attn_br, w_ssm_br, li, bsz)
    x2, h2 = _out_projection(merged, w_out, x2, mod_l, norm_mlp.reshape(1, d), li, bsz)
    return _mlp(h2, w_mlp_in, w_mlp_out, x2, mod_l, final_gain, li, bsz, final_norm)


def kernel(x, c, w_mod, b_mod, norm_mix, w_in, b_gate, conv_w, conv_b, a_log, dt_bias, d_skip, ssm_norm,
           w_attn_br, w_ssm_br, w_out, norm_mlp, w_mlp_in, w_mlp_out, norm_final):
    bsz, seq, d = x.shape
    depth = w_mod.shape[0]
    mod = _modulation(c, w_mod, b_mod).reshape(depth, bsz, 6, d)
    x2 = x.reshape(bsz * seq, d)
    final_gain = norm_final.reshape(1, d)
    w_main = jnp.concatenate([w_in[:, :, OFF_XBC:OFF_DT], w_in[:, :, OFF_GATE:], w_in[:, :, OFF_Z:OFF_XBC],
                              w_in[:, :, :QKV_COLS]], axis=2).astype(BF16)
    w_dt = jnp.pad(w_in[:, :, OFF_DT:OFF_GATE], ((0, 0), (0, 0), (0, LANES - 2 * SSM_HEADS)))
    w_dt_hi = w_dt.astype(BF16)
    w_dt_lo = (w_dt - w_dt_hi.astype(F32)).astype(BF16)
    w_dt2 = jnp.concatenate([w_dt_hi, w_dt_lo], axis=2)
    w_attn_br, w_ssm_br, w_out, w_mlp_in, w_mlp_out = (
        w.astype(BF16) for w in (w_attn_br, w_ssm_br, w_out, w_mlp_in, w_mlp_out))
    for i in range(depth):
        x2 = _layer(x2, bsz, seq, i, mod[i], norm_mix[i], w_main, w_dt2, b_gate[i], conv_w[i], conv_b[i],
                    a_log[i], dt_bias[i], d_skip[i], ssm_norm[i], w_attn_br, w_ssm_br, w_out, norm_mlp[i],
                    w_mlp_in, w_mlp_out, final_gain, i == depth - 1)
    return x2.reshape(bsz, seq, d)
```

```python
import functools
import math

import numpy as np
import jax
import jax.numpy as jnp
from jax import lax
from jax.experimental import pallas as pl
from jax.experimental.pallas import tpu as pltpu

F32 = jnp.float32
BF16 = jnp.bfloat16

D_MODEL = 2048
ATTN_GROUPS = ((128, 1), (512, 4), (2048, 16))
ATTN_HEADS = 8
ATTN_HEAD_DIM = 128
ATTN_WIDTH = ATTN_HEADS * ATTN_HEAD_DIM
ATTN_HALF = 64
ALIBI_SLOPES = tuple(2.0 ** (-8.0 * (j + 1) / ATTN_HEADS) for j in range(ATTN_HEADS))
SSM_INNER = D_MODEL
SSM_HEAD_DIM = 64
SSM_HEADS = SSM_INNER // SSM_HEAD_DIM
SSM_GROUPS = 8
SSM_HEADS_PER_GROUP = SSM_HEADS // SSM_GROUPS
SSM_STATE = 128
SSM_CONV = 5
XBC_WIDTH = SSM_INNER + 2 * SSM_GROUPS * SSM_STATE
GROUP_WIDTH = SSM_HEADS_PER_GROUP * SSM_HEAD_DIM
MLP_HIDDEN = 4 * D_MODEL
NORM_EPS = 1e-6
QKV_GROUP_COLS = 3 * ATTN_WIDTH
QKV_COLS = len(ATTN_GROUPS) * QKV_GROUP_COLS
OFF_Z = QKV_COLS
OFF_XBC = OFF_Z + SSM_INNER
OFF_DT = OFF_XBC + XBC_WIDTH
OFF_GATE = OFF_DT + 2 * SSM_HEADS
IN_WIDTH = OFF_GATE + 2 * D_MODEL

LANES = 128
SSD_CHUNK = 128
CONV_HALO = 16
NEG_BIG = -1e30
LOG2E = math.log2(math.e)
LN2 = math.log(2.0)
VMEM_LIMIT = 56 * 1024 * 1024

NAT_XBC = 0
NAT_GATE = NAT_XBC + XBC_WIDTH
NAT_Z = NAT_GATE + 2 * D_MODEL
NAT_QKV = NAT_Z + SSM_INNER
NAT_WIDTH = NAT_QKV + QKV_GROUP_COLS


def _params(sem, vmem=VMEM_LIMIT):
    return pltpu.CompilerParams(dimension_semantics=sem, vmem_limit_bytes=vmem)


def _dot(a, b):
    return jnp.dot(a, b, preferred_element_type=F32)


def _dot_nt(a, b):
    return lax.dot_general(a, b, (((1,), (1,)), ((), ())), preferred_element_type=F32)


def _dot_tn(a, b):
    return lax.dot_general(a, b, (((0,), (0,)), ((), ())), preferred_element_type=F32)


def _dot_exact(a, b):
    return jnp.dot(a, b, preferred_element_type=F32, precision=lax.Precision.HIGHEST)


def _resident(shape):
    return pl.BlockSpec(shape, lambda *_: (0,) * len(shape), pipeline_mode=pl.Buffered(1))


def _resident_layer(shape, li):
    return pl.BlockSpec((None,) + tuple(shape), lambda *_: (li,) + (0,) * len(shape),
                        pipeline_mode=pl.Buffered(1))


def _mod_kernel(ct_ref, w_ref, b_ref, o_ref):
    bsz = o_ref.shape[0]
    w = w_ref[...]
    rows = [jnp.sum(w * ct_ref[:, b:b + 1], axis=0, keepdims=True) for b in range(bsz)]
    o_ref[...] = jnp.concatenate(rows, axis=0) + b_ref[...]


def _modulation(c, w_mod, b_mod):
    depth, d, n = w_mod.shape
    bsz = c.shape[0]
    tn = 2048
    return pl.pallas_call(
        _mod_kernel,
        out_shape=jax.ShapeDtypeStruct((depth, bsz, n), F32),
        grid=(depth, n // tn),
        in_specs=[
            pl.BlockSpec((d, bsz), lambda l, j: (0, 0)),
            pl.BlockSpec((None, d, tn), lambda l, j: (l, 0, j)),
            pl.BlockSpec((None, 1, tn), lambda l, j: (l, 0, j)),
        ],
        out_specs=pl.BlockSpec((None, bsz, tn), lambda l, j: (l, 0, j)),
        compiler_params=_params(("parallel", "parallel")),
        name="adaln_mod",
    )(c.T, w_mod, b_mod.reshape(depth, 1, n))


_PACK_TN = 1024
_PACK_ROWS = 512
_PACK_XBC = XBC_WIDTH // _PACK_TN
_PACK_GATE = _PACK_XBC + 2 * D_MODEL // _PACK_TN
_PACK_Z = _PACK_GATE + SSM_INNER // _PACK_TN
_PACK_SHIFT = OFF_GATE % _PACK_TN


def _pack_src_tile(j):
    return jnp.where(j < _PACK_XBC, OFF_XBC // _PACK_TN + j,
                     jnp.where(j < _PACK_GATE, OFF_GATE // _PACK_TN + j - _PACK_XBC,
                               jnp.where(j < _PACK_Z, OFF_Z // _PACK_TN + j - _PACK_GATE, j - _PACK_Z)))


def _pack_kernel(a_ref, b_ref, o_ref):
    j = pl.program_id(1)
    is_gate = (j >= _PACK_XBC) & (j < _PACK_GATE)

    @pl.when(jnp.logical_not(is_gate))
    def _():
        o_ref[...] = a_ref[...].astype(BF16)

    @pl.when(is_gate)
    def _():
        sh = _PACK_SHIFT
        o_ref[...] = jnp.concatenate([a_ref[:, sh:], b_ref[:, :sh]], axis=1).astype(BF16)


def _pack_in_weights(w_in):
    depth, d, _ = w_in.shape
    tn, tr = _PACK_TN, _PACK_ROWS
    b_idle = OFF_GATE // tn + 1
    return pl.pallas_call(
        _pack_kernel,
        out_shape=jax.ShapeDtypeStruct((depth, d, _IN_TILES * tn), BF16),
        grid=(depth, _IN_TILES, d // tr),
        in_specs=[
            pl.BlockSpec((None, tr, tn), lambda l, j, r: (l, r, _pack_src_tile(j))),
            pl.BlockSpec((None, tr, tn), lambda l, j, r: (
                l, jnp.where((j >= _PACK_XBC) & (j < _PACK_GATE), r, 0),
                jnp.where((j >= _PACK_XBC) & (j < _PACK_GATE), _pack_src_tile(j) + 1, b_idle))),
        ],
        out_specs=pl.BlockSpec((None, tr, tn), lambda l, j, r: (l, r, j)),
        compiler_params=_params(("parallel", "parallel", "parallel")),
        name="pack_in_weights",
    )(w_in, w_in)


_IN_TM = 1024
_IN_TN = 1024
_NAT_TILES = NAT_WIDTH // _IN_TN
_QKV_TILES = QKV_GROUP_COLS // _IN_TN
_IN_TILES = _NAT_TILES + 2 * _QKV_TILES
_NORM_ROWS = 256
_DEINTERLEAVE_STRIDE = 4


def _modulated_norm(x_ref, gain_ref, shift, scale, r0, rows):
    xf = x_ref[pl.ds(r0, rows), :]
    ms = jnp.mean(xf * xf, axis=-1, keepdims=True)
    y = xf * lax.rsqrt(ms + NORM_EPS) * gain_ref[...]
    return y * (1.0 + scale) + shift


def _inproj_kernel(x_ref, mod_ref, gain_ref, w_ref, wdt_ref, nat_ref, q1_ref, q2_ref, dt_ref, h_scr, r_scr,
                   r2_scr):
    j = pl.program_id(1)
    tm = x_ref.shape[0]

    @pl.when(j == 0)
    def _():
        shift = mod_ref[0:1, :]
        scale = mod_ref[1:2, :]
        for r in range(tm // _NORM_ROWS):
            r0 = r * _NORM_ROWS
            h = _modulated_norm(x_ref, gain_ref, shift, scale, r0, _NORM_ROWS)
            h_hi = h.astype(BF16)
            h_lo = (h - h_hi.astype(F32)).astype(BF16)
            h_scr[pl.ds(r0, _NORM_ROWS), :] = h_hi
            both = _dot(h_hi, wdt_ref[...])
            dt_ref[pl.ds(r0, _NORM_ROWS), :] = (both[:, :LANES] + both[:, LANES:]
                                                + _dot(h_lo, wdt_ref[:, :LANES]))

    @pl.when(j < _NAT_TILES)
    def _():
        nat_ref[...] = _dot(h_scr[...], w_ref[...]).astype(BF16)

    for lo, ref in ((_NAT_TILES, q1_ref), (_NAT_TILES + _QKV_TILES, q2_ref)):
        @pl.when((j >= lo) & (j < lo + _QKV_TILES))
        def _(ref=ref):
            dil, rows = ref.shape[0], ref.shape[1]
            res = _dot(h_scr[...], w_ref[...])
            for cblk in range(_IN_TN // LANES):
                r_scr[cblk] = res[:, cblk * LANES:(cblk + 1) * LANES]
            if dil <= _DEINTERLEAVE_STRIDE:
                for r in range(dil):
                    for cblk in range(_IN_TN // LANES):
                        ref[r, :, cblk * LANES:(cblk + 1) * LANES] = (
                            r_scr[cblk, pl.ds(r, rows, stride=dil), :].astype(BF16))
            else:
                st = _DEINTERLEAVE_STRIDE
                outer = dil // st
                mid = tm // st
                for cblk in range(_IN_TN // LANES):
                    for b in range(st):
                        r2_scr[cblk * st + b] = r_scr[cblk, pl.ds(b, mid, stride=st), :]
                for a in range(outer):
                    for b in range(st):
                        for cblk in range(_IN_TN // LANES):
                            ref[a * st + b, :, cblk * LANES:(cblk + 1) * LANES] = (
                                r2_scr[cblk * st + b, pl.ds(a, rows, stride=outer), :].astype(BF16))


def _in_projection(x2, mod_l, gain, w_main, w_dt, li, bsz):
    t, d = x2.shape
    seq = t // bsz
    tm = min(_IN_TM, seq)
    tpb = seq // tm
    tn = _IN_TN
    d1, d2 = ATTN_GROUPS[1][1], ATTN_GROUPS[2][1]

    def dil_spec(dil, lo):
        return pl.BlockSpec((None, dil, tm // dil, tn),
                            lambda i, j: (i // tpb, 0, i % tpb, jnp.clip(j - lo, 0, _QKV_TILES - 1)))

    return pl.pallas_call(
        _inproj_kernel,
        out_shape=[jax.ShapeDtypeStruct((t, NAT_WIDTH), BF16),
                   jax.ShapeDtypeStruct((bsz, d1, seq // d1, QKV_GROUP_COLS), BF16),
                   jax.ShapeDtypeStruct((bsz, d2, seq // d2, QKV_GROUP_COLS), BF16),
                   jax.ShapeDtypeStruct((t, LANES), F32)],
        grid=(t // tm, _IN_TILES),
        in_specs=[
            pl.BlockSpec((tm, d), lambda i, j: (i, 0)),
            pl.BlockSpec((None, 6, d), lambda i, j: (i // tpb, 0, 0)),
            _resident((1, d)),
            pl.BlockSpec((None, d, tn), lambda i, j: (li, 0, j)),
            _resident_layer((d, 2 * LANES), li),
        ],
        out_specs=[pl.BlockSpec((tm, tn), lambda i, j: (i, jnp.minimum(j, _NAT_TILES - 1))),
                   dil_spec(d1, _NAT_TILES),
                   dil_spec(d2, _NAT_TILES + _QKV_TILES),
                   pl.BlockSpec((tm, LANES), lambda i, j: (i, 0))],
        scratch_shapes=[pltpu.VMEM((tm, d), BF16), pltpu.VMEM((tn // LANES, tm, LANES), F32),
                        pltpu.VMEM((tn // LANES * _DEINTERLEAVE_STRIDE, tm // _DEINTERLEAVE_STRIDE, LANES), F32)],
        compiler_params=_params(("parallel", "arbitrary")),
        name="in_projection",
    )(x2, mod_l, gain, w_main, w_dt)


_ATTN_QB = 128
_ATTN_NSUB = 4


def _alibi_table(dil, qb):
    half = ATTN_HALF
    row = np.arange(qb)[:, None]
    col = np.arange(qb)[None, :]
    dist_c = np.abs(col - row)
    hcol = np.arange(2 * half)[None, :]
    dist_p = row + half - hcol
    dist_n = (hcol - half) + qb - row
    dist_h = np.where(hcol < half, dist_p, dist_n)
    dist = np.concatenate([dist_c, dist_h], axis=1).astype(np.float64)
    ok = dist <= half
    slopes = np.asarray(ALIBI_SLOPES, np.float64)[:, None, None]
    tbl = np.where(ok[None], -slopes * dist[None] * dil * LOG2E, NEG_BIG)
    return jnp.asarray(tbl, F32)


def _attn_kernel(q_ref, kc_ref, vc_ref, kp_ref, vp_ref, kn_ref, vn_ref, bias_ref, o_ref, lse_ref,
                 kh_scr, vh_scr, *, nblk):
    j = pl.program_id(2)
    qb = _ATTN_QB
    nsub = q_ref.shape[0] // qb
    half = ATTN_HALF
    c1 = ATTN_HEAD_DIM ** -0.5 * LOG2E
    hcol = lax.broadcasted_iota(jnp.int32, (qb, 2 * half), 1)
    lane = lax.broadcasted_iota(jnp.int32, (qb, LANES), 1)
    ones = jnp.ones((qb, LANES), BF16)
    for s in range(nsub):
        rows = pl.ds(s * qb, qb)
        edge_ok = None
        if s == 0:
            kh_scr[s, 0:half, :] = kp_ref[...]
            vh_scr[s, 0:half, :] = vp_ref[...]
            edge_ok = hcol >= jnp.where(j > 0, 0, half)
        else:
            kh_scr[s, 0:half, :] = kc_ref[pl.ds(s * qb - half, half), :]
            vh_scr[s, 0:half, :] = vc_ref[pl.ds(s * qb - half, half), :]
        if s == nsub - 1:
            kh_scr[s, half:2 * half, :] = kn_ref[...]
            vh_scr[s, half:2 * half, :] = vn_ref[...]
            ok_n = hcol < jnp.where(j < nblk - 1, 2 * half, half)
            edge_ok = ok_n if edge_ok is None else edge_ok & ok_n
        else:
            kh_scr[s, half:2 * half, :] = kc_ref[pl.ds((s + 1) * qb, half), :]
            vh_scr[s, half:2 * half, :] = vc_ref[pl.ds((s + 1) * qb, half), :]
        lse_all = jnp.zeros((qb, LANES), F32)
        for h in range(ATTN_HEADS):
            sl = slice(h * ATTN_HEAD_DIM, (h + 1) * ATTN_HEAD_DIM)
            q = q_ref[rows, sl]
            t_c = _dot_nt(q, kc_ref[rows, sl]) * c1 + bias_ref[h, :, 0:qb]
            t_h = _dot_nt(q, kh_scr[s, :, sl]) * c1 + bias_ref[h, :, qb:]
            if edge_ok is not None:
                t_h = jnp.where(edge_ok, t_h, NEG_BIG)
            m = jnp.max(jnp.maximum(t_c, t_h), axis=-1, keepdims=True)
            p_c = jnp.exp2(t_c - m).astype(BF16)
            p_h = jnp.exp2(t_h - m).astype(BF16)
            acc = (_dot(p_c, jnp.concatenate([vc_ref[rows, sl], ones], axis=1))
                   + _dot(p_h, jnp.concatenate([vh_scr[s, :, sl], ones], axis=1)))
            l = acc[:, LANES:LANES + 1]
            o_ref[rows, sl] = (acc[:, :LANES] * (1.0 / l)).astype(BF16)
            lse_all = jnp.where(lane == h, (m + jnp.log2(l)) * LN2, lse_all)
        lse_ref[rows, :] = lse_all


def _attention_group(qkv, part0, dil):
    bsz, _, n, _ = qkv.shape
    nsub = min(_ATTN_NSUB, n // _ATTN_QB)
    qb = nsub * _ATTN_QB
    nblk = n // qb
    half = ATTN_HALF
    hb = qb // half
    nhalf = n // half
    w = ATTN_WIDTH
    cur = lambda part: pl.BlockSpec((None, None, qb, w), lambda b, r, j: (b, r, j, part0 + part))
    prev = lambda part: pl.BlockSpec(
        (None, None, half, w), lambda b, r, j: (b, r, jnp.maximum(j * hb - 1, 0), part0 + part))
    nxt = lambda part: pl.BlockSpec(
        (None, None, half, w), lambda b, r, j: (b, r, jnp.minimum((j + 1) * hb, nhalf - 1), part0 + part))
    return pl.pallas_call(
        functools.partial(_attn_kernel, nblk=nblk),
        out_shape=[jax.ShapeDtypeStruct((bsz, dil, n, w), BF16),
                   jax.ShapeDtypeStruct((bsz, dil, n, LANES), F32)],
        grid=(bsz, dil, nblk),
        in_specs=[cur(0), cur(1), cur(2), prev(1), prev(2), nxt(1), nxt(2),
                  _resident((ATTN_HEADS, _ATTN_QB, 2 * _ATTN_QB))],
        out_specs=[pl.BlockSpec((None, None, qb, w), lambda b, r, j: (b, r, j, 0)),
                   pl.BlockSpec((None, None, qb, LANES), lambda b, r, j: (b, r, j, 0))],
        scratch_shapes=[pltpu.VMEM((nsub, 2 * half, w), BF16), pltpu.VMEM((nsub, 2 * half, w), BF16)],
        compiler_params=_params(("parallel", "parallel", "parallel")),
        name=f"band_attention_d{dil}",
    )(qkv, qkv, qkv, qkv, qkv, qkv, qkv, _alibi_table(dil, _ATTN_QB))


def _softplus(x):
    return jnp.maximum(x, 0.0) + jnp.log(1.0 + jnp.exp(-jnp.abs(x)))


def _head_expansion_matrix(first_lane):
    e = np.zeros((LANES, SSM_INNER), np.float32)
    for h in range(SSM_HEADS):
        e[first_lane + h, h * SSM_HEAD_DIM:(h + 1) * SSM_HEAD_DIM] = 1.0
    return jnp.asarray(np.concatenate([e, e, e], axis=0), BF16)


def _expand_heads(q, e3_ref):
    return _dot(jnp.concatenate(_split3(q), axis=1), e3_ref[...])


def _dt_terms(dt_ref, par_ref):
    a_row = -jnp.exp(par_ref[0:1, :])
    dtv = _softplus(dt_ref[...] + par_ref[1:2, :])
    return dtv, dtv * a_row


def _ssd_fwd_kernel(xm_ref, xp_ref, xn_ref, dt_ref, cw_ref, cb_ref, par_ref, dsk_ref, e3_ref,
                    xc_ref, y1_ref, xf_scr, st_scr, *, nc):
    c = pl.program_id(1)
    L = SSD_CHUNK
    H = SSM_HEADS
    R = SSM_HEADS_PER_GROUP
    N = SSM_STATE
    GW = GROUP_WIDTH

    @pl.when(c == 0)
    def _():
        st_scr[...] = jnp.zeros_like(st_scr)

    has_prev = c > 0
    has_next = c < nc - 1
    cwid = 512
    for cc in range(XBC_WIDTH // cwid):
        cols = slice(cc * cwid, (cc + 1) * cwid)
        xm = xm_ref[:, cols].astype(F32)
        xp = jnp.where(has_prev, xp_ref[:, cols].astype(F32), 0.0)
        xn = jnp.where(has_next, xn_ref[:, cols].astype(F32), 0.0)
        xcat = jnp.concatenate([xp, xm, xn], axis=0)
        rows_cat = L + 2 * CONV_HALO
        acc = cb_ref[:, cols] + cw_ref[SSM_CONV // 2:SSM_CONV // 2 + 1, cols] * xm
        for k in range(SSM_CONV):
            if k != SSM_CONV // 2:
                shifted = pltpu.roll(xcat, (SSM_CONV // 2 - k) % rows_cat, axis=0)[CONV_HALO:CONV_HALO + L]
                acc = acc + cw_ref[k:k + 1, cols] * shifted
        sv = acc * (1.0 / (1.0 + jnp.exp(-acc)))
        xf_scr[:, cols] = sv
        xc_ref[:, cols] = sv.astype(BF16)

    dtv, dta = _dt_terms(dt_ref, par_ref)
    ti = lax.broadcasted_iota(jnp.int32, (L, L), 0)
    si = lax.broadcasted_iota(jnp.int32, (L, L), 1)
    lower = si <= ti
    strict_lower = si < ti
    strict_upper = si > ti
    a_f = _dot_exact(lower.astype(F32), dta)
    g_b = _dot_exact((si >= ti).astype(F32), dta)
    a_f_t = a_f.T
    g_b_t = g_b.T
    dt_t = dtv.T
    exp_af = jnp.exp(a_f)
    a_end = a_f[L - 1:L, :]
    w_end = jnp.exp(a_end - a_f) * dtv
    cd_row = jnp.exp(a_end)
    lane = lax.broadcasted_iota(jnp.int32, (L, GW), 1)
    fac = _expand_heads(jnp.concatenate([exp_af, w_end, jnp.broadcast_to(cd_row, (8, LANES))], axis=0), e3_ref)

    for g in range(SSM_GROUPS):
        gsl = slice(g * GW, (g + 1) * GW)
        xg = xf_scr[:, gsl]
        xg16 = xg.astype(BF16)
        bg = xf_scr[:, SSM_INNER + g * N:SSM_INNER + (g + 1) * N]
        cg16 = xf_scr[:, SSM_INNER + SSM_GROUPS * N + g * N:
                      SSM_INNER + SSM_GROUPS * N + (g + 1) * N].astype(BF16)
        bg16 = bg.astype(BF16)
        cb = _dot_nt(cg16, bg16)
        ms, xs = [], []
        for r in range(R):
            h = g * R + r
            e = jnp.where(lower, a_f[:, h:h + 1] - a_f_t[h:h + 1, :],
                          g_b[:, H + h:H + h + 1] - g_b_t[H + h:H + h + 1, :])
            dtf_row = dt_t[h:h + 1, :]
            dtb_row = dt_t[H + h:H + h + 1, :]
            dsel = jnp.where(strict_lower, dtf_row, jnp.where(strict_upper, dtb_row, dtf_row + dtb_row))
            ms.append((cb * (jnp.exp(e) * dsel)).astype(BF16))
            xs.append(jnp.where((lane >= r * SSM_HEAD_DIM) & (lane < (r + 1) * SSM_HEAD_DIM),
                                xg16, jnp.zeros_like(xg16)))
        y = _dot(jnp.concatenate(ms, axis=1), jnp.concatenate(xs, axis=0))
        st = st_scr[g]
        y = y + _dot(cg16, st.astype(BF16)) * fac[0:L, gsl]
        y = y + xg * dsk_ref[:, gsl]
        y1_ref[:, gsl] = y.astype(BF16)
        xw = (xg * fac[L:2 * L, gsl]).astype(BF16)
        s_new = _dot_tn(bg16, xw)
        st_scr[g] = st * fac[2 * L:2 * L + 1, gsl] + s_new


def _ssd_bwd_kernel(xc_ref, dt_ref, par_ref, y1_ref, z_ref, ng_ref, e3_ref, y_ref, st_scr):
    c = pl.program_id(1)
    L = SSD_CHUNK
    H = SSM_HEADS
    R = SSM_HEADS_PER_GROUP
    N = SSM_STATE
    GW = GROUP_WIDTH

    @pl.when(c == 0)
    def _():
        st_scr[...] = jnp.zeros_like(st_scr)

    dtv, dta = _dt_terms(dt_ref, par_ref)
    ti = lax.broadcasted_iota(jnp.int32, (L, L), 0)
    si = lax.broadcasted_iota(jnp.int32, (L, L), 1)
    g_b = _dot_exact((si >= ti).astype(F32), dta)
    exp_gb = jnp.exp(g_b)
    g_start = g_b[0:1, :]
    w_start = jnp.exp(g_start - g_b) * dtv
    cd_row = jnp.exp(g_start)
    fac = _expand_heads(jnp.concatenate([exp_gb, w_start, jnp.broadcast_to(cd_row, (8, LANES))], axis=0), e3_ref)

    for g in range(SSM_GROUPS):
        gsl = slice(g * GW, (g + 1) * GW)
        xg = xc_ref[:, gsl].astype(F32)
        bg16 = xc_ref[:, SSM_INNER + g * N:SSM_INNER + (g + 1) * N]
        cg16 = xc_ref[:, SSM_INNER + SSM_GROUPS * N + g * N:SSM_INNER + SSM_GROUPS * N + (g + 1) * N]
        st = st_scr[g]
        y = y1_ref[:, gsl].astype(F32) + _dot(cg16, st.astype(BF16)) * fac[0:L, gsl]
        zg = z_ref[:, gsl].astype(F32)
        y = y * (zg * (1.0 / (1.0 + jnp.exp(-zg))))
        ms = jnp.mean(y * y, axis=-1, keepdims=True)
        y_ref[:, gsl] = (y * lax.rsqrt(ms + NORM_EPS) * ng_ref[:, gsl]).astype(BF16)
        xw = (xg * fac[L:2 * L, gsl]).astype(BF16)
        s_new = _dot_tn(bg16, xw)
        st_scr[g] = st * fac[2 * L:2 * L + 1, gsl] + s_new


def _bidirectional_ssd(nat, dt, conv_w, conv_b, ssm_par, d_skip_x, norm_g, bsz, seq):
    L = SSD_CHUNK
    nc = seq // L
    hpc = L // CONV_HALO
    nhalo = seq // CONV_HALO
    nat3 = nat.reshape(bsz, seq, NAT_WIDTH)
    dt3 = dt.reshape(bsz, seq, LANES)
    xbc_blk = NAT_XBC // XBC_WIDTH
    z_blk = NAT_Z // SSM_INNER
    state = pltpu.VMEM((SSM_GROUPS, SSM_STATE, GROUP_WIDTH), F32)
    xc, y1 = pl.pallas_call(
        functools.partial(_ssd_fwd_kernel, nc=nc),
        out_shape=[jax.ShapeDtypeStruct((bsz, seq, XBC_WIDTH), BF16),
                   jax.ShapeDtypeStruct((bsz, seq, SSM_INNER), BF16)],
        grid=(bsz, nc),
        in_specs=[
            pl.BlockSpec((None, L, XBC_WIDTH), lambda b, c: (b, c, xbc_blk)),
            pl.BlockSpec((None, CONV_HALO, XBC_WIDTH), lambda b, c: (b, jnp.maximum(c * hpc - 1, 0), xbc_blk)),
            pl.BlockSpec((None, CONV_HALO, XBC_WIDTH),
                         lambda b, c: (b, jnp.minimum((c + 1) * hpc, nhalo - 1), xbc_blk)),
            pl.BlockSpec((None, L, LANES), lambda b, c: (b, c, 0)),
            _resident((SSM_CONV, XBC_WIDTH)), _resident((1, XBC_WIDTH)), _resident((8, LANES)),
            _resident((1, SSM_INNER)), _resident((3 * LANES, SSM_INNER)),
        ],
        out_specs=[pl.BlockSpec((None, L, XBC_WIDTH), lambda b, c: (b, c, 0)),
                   pl.BlockSpec((None, L, SSM_INNER), lambda b, c: (b, c, 0))],
        scratch_shapes=[pltpu.VMEM((L, XBC_WIDTH), F32), state],
        compiler_params=_params(("parallel", "arbitrary")),
        name="ssd_forward_sweep",
    )(nat3, nat3, nat3, dt3, conv_w, conv_b, ssm_par, d_skip_x, _head_expansion_matrix(0))
    rev = lambda width, blk=0: pl.BlockSpec((None, L, width), lambda b, c: (b, nc - 1 - c, blk))
    y = pl.pallas_call(
        _ssd_bwd_kernel,
        out_shape=jax.ShapeDtypeStruct((bsz, seq, SSM_INNER), BF16),
        grid=(bsz, nc),
        in_specs=[rev(XBC_WIDTH), rev(LANES), _resident((8, LANES)), rev(SSM_INNER),
                  rev(SSM_INNER, z_blk), _resident((1, SSM_INNER)), _resident((3 * LANES, SSM_INNER))],
        out_specs=rev(SSM_INNER),
        scratch_shapes=[state],
        compiler_params=_params(("parallel", "arbitrary")),
        name="ssd_backward_sweep",
    )(xc, dt3, ssm_par, y1, nat3, norm_g, _head_expansion_matrix(SSM_HEADS))
    return y.reshape(bsz * seq, SSM_INNER)


_MERGE_TM = 512
_MERGE_NB = 512
_MERGE_PB = 256


def _interleave_matrix(tm, dil):
    p = np.zeros((tm, tm), np.float32)
    t = np.arange(tm)
    p[t, (t % dil) * (tm // dil) + t // dil] = 1.0
    return jnp.asarray(p, BF16)


def _split3(x):
    hi = x.astype(BF16)
    r1 = x - hi.astype(F32)
    mid = r1.astype(BF16)
    lo = (r1 - mid.astype(F32)).astype(BF16)
    return hi, mid, lo


def _merge_kernel(o0_ref, o1_ref, o2_ref, l0_ref, l1_ref, l2_ref, p1_ref, p2_ref, ys_ref, gate_ref, bg_ref,
                  wa_ref, ws_ref, out_ref, ya_scr, s_scr):
    tm, d = out_ref.shape

    def token_order(o_ref, l_ref, p_ref):
        dil = o_ref.shape[0]
        pb = p_ref.shape[0]
        rpb = pb // dil
        w = o_ref.shape[2]
        o_tok, l_tok = [], []
        for blk in range(tm // pb):
            rs = slice(blk * rpb, (blk + 1) * rpb)
            flat = jnp.concatenate(
                [jnp.concatenate((o_ref[r, rs, :],) + _split3(l_ref[r, rs, :]), axis=1) for r in range(dil)],
                axis=0)
            res = _dot(p_ref[...], flat)
            o_tok.append(res[:, :w])
            l_tok.append(res[:, w:w + LANES] + res[:, w + LANES:w + 2 * LANES] + res[:, w + 2 * LANES:])
        return jnp.concatenate(o_tok, axis=0), jnp.concatenate(l_tok, axis=0)

    o1, l1 = token_order(o1_ref, l1_ref, p1_ref)
    o2, l2 = token_order(o2_ref, l2_ref, p2_ref)
    nb = _MERGE_NB
    for cb in range(d // nb):
        cs = slice(cb * nb, (cb + 1) * nb)
        s_scr[:, cs] = _dot(ys_ref[...], ws_ref[:, cs])
    l0 = l0_ref[...]
    m = jnp.maximum(l0, jnp.maximum(l1, l2))
    e0, e1, e2 = jnp.exp(l0 - m), jnp.exp(l1 - m), jnp.exp(l2 - m)
    inv = 1.0 / (e0 + e1 + e2)
    w0, w1, w2 = e0 * inv, e1 * inv, e2 * inv
    for h in range(ATTN_HEADS):
        sl = slice(h * ATTN_HEAD_DIM, (h + 1) * ATTN_HEAD_DIM)
        ya_scr[:, sl] = (w0[:, h:h + 1] * o0_ref[:, sl].astype(F32) + w1[:, h:h + 1] * o1[:, sl]
                         + w2[:, h:h + 1] * o2[:, sl]).astype(BF16)
    for cb in range(d // nb):
        cs = slice(cb * nb, (cb + 1) * nb)
        cs2 = slice(d + cb * nb, d + (cb + 1) * nb)
        ga = gate_ref[:, cs].astype(F32) + bg_ref[:, cs]
        gs = gate_ref[:, cs2].astype(F32) + bg_ref[:, cs2]
        a = _dot(ya_scr[...], wa_ref[:, cs])
        s = s_scr[:, cs]
        out_ref[:, cs] = (a * (1.0 / (1.0 + jnp.exp(-ga))) + s * (1.0 / (1.0 + jnp.exp(-gs)))).astype(BF16)


def _branch_merge(o0, o1, o2, l0, l1, l2, y_ssm, nat, b_gate, w_attn_br, w_ssm_br, li, bsz):
    t = y_ssm.shape[0]
    d = D_MODEL
    seq = t // bsz
    tm = min(_MERGE_TM, seq // ATTN_GROUPS[2][1])
    tpb = seq // tm
    pb = min(_MERGE_PB, tm)
    row = lambda width, blk=0: pl.BlockSpec((tm, width), lambda i: (i, blk))

    def dil_spec(arr):
        dil, width = arr.shape[1], arr.shape[3]
        return pl.BlockSpec((None, dil, tm // dil, width), lambda i: (i // tpb, 0, i % tpb, 0))

    return pl.pallas_call(
        _merge_kernel,
        out_shape=jax.ShapeDtypeStruct((t, d), BF16),
        grid=(t // tm,),
        in_specs=[row(ATTN_WIDTH), dil_spec(o1), dil_spec(o2), row(LANES), dil_spec(l1), dil_spec(l2),
                  _resident((pb, pb)), _resident((pb, pb)),
                  row(SSM_INNER), row(2 * d, NAT_GATE // (2 * d)), _resident((1, 2 * d)),
                  _resident_layer((ATTN_WIDTH, d), li), _resident_layer((SSM_INNER, d), li)],
        out_specs=row(d),
        scratch_shapes=[pltpu.VMEM((tm, ATTN_WIDTH), BF16), pltpu.VMEM((tm, d), F32)],
        compiler_params=_params(("parallel",)),
        name="branch_merge",
    )(o0, o1, o2, l0, l1, l2, _interleave_matrix(pb, o1.shape[1]), _interleave_matrix(pb, o2.shape[1]),
      y_ssm, nat, b_gate, w_attn_br, w_ssm_br)


def _outproj_kernel(mg_ref, w_ref, x_ref, mod_ref, gain_ref, xo_ref, h2_ref):
    xn = x_ref[...] + mod_ref[2:3, :] * _dot(mg_ref[...], w_ref[...])
    xo_ref[...] = xn
    ms = jnp.mean(xn * xn, axis=-1, keepdims=True)
    y = xn * lax.rsqrt(ms + NORM_EPS) * gain_ref[...]
    h2_ref[...] = (y * (1.0 + mod_ref[4:5, :]) + mod_ref[3:4, :]).astype(BF16)


def _out_projection(merged, w_out, x2, mod_l, gain, li, bsz):
    t, d = x2.shape
    seq = t // bsz
    tm = min(512, seq)
    tpb = seq // tm
    row = lambda: pl.BlockSpec((tm, d), lambda i: (i, 0))
    return pl.pallas_call(
        _outproj_kernel,
        out_shape=[jax.ShapeDtypeStruct((t, d), F32), jax.ShapeDtypeStruct((t, d), BF16)],
        grid=(t // tm,),
        in_specs=[row(), _resident_layer((d, d), li), row(),
                  pl.BlockSpec((None, 6, d), lambda i: (i // tpb, 0, 0)),
                  _resident((1, d))],
        out_specs=[row(), row()],
        compiler_params=_params(("parallel",)),
        name="out_projection",
    )(merged, w_out, x2, mod_l, gain)


_MLP_TM = 1024
_MLP_TK = 512
_MLP_NB = 512
_MLP_ROWS = 256


def _mlp_kernel(h_ref, w1_ref, w2_ref, x_ref, mod_ref, fg_ref, o_ref, *, nk, final_norm):
    k = pl.program_id(1)
    tm, d = o_ref.shape

    @pl.when(k == 0)
    def _():
        o_ref[...] = jnp.zeros_like(o_ref)

    u = jnp.maximum(_dot(h_ref[...], w1_ref[...]), 0.0)
    u2 = (u * u).astype(BF16)
    for cb in range(d // _MLP_NB):
        cs = slice(cb * _MLP_NB, (cb + 1) * _MLP_NB)
        o_ref[:, cs] += _dot(u2, w2_ref[:, cs])

    @pl.when(k == nk - 1)
    def _():
        for r in range(tm // _MLP_ROWS):
            rs = pl.ds(r * _MLP_ROWS, _MLP_ROWS)
            xn = x_ref[rs, :] + mod_ref[5:6, :] * o_ref[rs, :]
            if final_norm:
                ms = jnp.mean(xn * xn, axis=-1, keepdims=True)
                xn = xn * lax.rsqrt(ms + NORM_EPS) * fg_ref[...]
            o_ref[rs, :] = xn


def _mlp(h2, w1, w2, x2, mod_l, final_gain, li, bsz, final_norm):
    t, d = x2.shape
    seq = t // bsz
    tm = min(_MLP_TM, seq)
    tk = _MLP_TK
    nk = w1.shape[2] // tk
    tpb = seq // tm
    return pl.pallas_call(
        functools.partial(_mlp_kernel, nk=nk, final_norm=final_norm),
        out_shape=jax.ShapeDtypeStruct((t, d), F32),
        grid=(t // tm, nk),
        in_specs=[pl.BlockSpec((tm, d), lambda i, k: (i, 0)),
                  pl.BlockSpec((None, d, tk), lambda i, k: (li, 0, k)),
                  pl.BlockSpec((None, tk, d), lambda i, k: (li, k, 0)),
                  pl.BlockSpec((tm, d), lambda i, k: (i, 0), pipeline_mode=pl.Buffered(1)),
                  pl.BlockSpec((None, 6, d), lambda i, k: (i // tpb, 0, 0)),
                  _resident((1, d))],
        out_specs=pl.BlockSpec((tm, d), lambda i, k: (i, 0)),
        compiler_params=_params(("parallel", "arbitrary")),
        name="relu2_mlp",
    )(h2, w1, w2, x2, mod_l, final_gain)


def _layer(x2, bsz, seq, li, mod_l, norm_mix, w_main, w_dt2, b_gate, conv_w, conv_b, a_log, dt_bias, d_skip,
           ssm_norm, w_attn_br, w_ssm_br, w_out, norm_mlp, w_mlp_in, w_mlp_out, final_gain, final_norm):
    d = D_MODEL
    ssm_par = jnp.pad(jnp.stack([a_log.reshape(-1), dt_bias.reshape(-1)]),
                      ((0, 6), (0, LANES - 2 * SSM_HEADS)))
    d_skip_x = jnp.repeat(d_skip, SSM_HEAD_DIM).reshape(1, SSM_INNER)

    nat, q1, q2, dt = _in_projection(x2, mod_l, norm_mix.reshape(1, d), w_main, w_dt2, li, bsz)
    nat4 = nat.reshape(bsz, 1, seq, NAT_WIDTH)
    o0, l0 = _attention_group(nat4, NAT_QKV // ATTN_WIDTH, 1)
    o1, l1 = _attention_group(q1, 0, ATTN_GROUPS[1][1])
    o2, l2 = _attention_group(q2, 0, ATTN_GROUPS[2][1])
    y_ssm = _bidirectional_ssd(nat, dt, conv_w.reshape(SSM_CONV, XBC_WIDTH), conv_b.reshape(1, XBC_WIDTH),
                               ssm_par, d_skip_x, ssm_norm.reshape(1, SSM_INNER), bsz, seq)
    merged = _branch_merge(o0.reshape(bsz * seq, ATTN_WIDTH), o1, o2, l0.reshape(bsz * seq, LANES), l1, l2,
                           y_ssm, nat, b_gate.reshape(1, 2 * d), w_attn_br, w_ssm_br, li, bsz)
    x2, h2 = _out_projection(merged, w_out, x2, mod_l, norm_mlp.reshape(1, d), li, bsz)
    return _mlp(h2, w_mlp_in, w_mlp_out, x2, mod_l, final_gain, li, bsz, final_norm)


def kernel(x, c, w_mod, b_mod, norm_mix, w_in, b_gate, conv_w, conv_b, a_log, dt_bias, d_skip, ssm_norm,
           w_attn_br, w_ssm_br, w_out, norm_mlp, w_mlp_in, w_mlp_out, norm_final):
    bsz, seq, d = x.shape
    depth = w_mod.shape[0]
    mod = _modulation(c, w_mod, b_mod).reshape(depth, bsz, 6, d)
    x2 = x.reshape(bsz * seq, d)
    final_gain = norm_final.reshape(1, d)
    w_main = _pack_in_weights(w_in)
    w_dt = jnp.pad(w_in[:, :, OFF_DT:OFF_GATE], ((0, 0), (0, 0), (0, LANES - 2 * SSM_HEADS)))
    w_dt_hi = w_dt.astype(BF16)
    w_dt_lo = (w_dt - w_dt_hi.astype(F32)).astype(BF16)
    w_dt2 = jnp.concatenate([w_dt_hi, w_dt_lo], axis=2)
    w_attn_br, w_ssm_br, w_out, w_mlp_in, w_mlp_out = (
        w.astype(BF16) for w in (w_attn_br, w_ssm_br, w_out, w_mlp_in, w_mlp_out))
    for i in range(depth):
        x2 = _layer(x2, bsz, seq, i, mod[i], norm_mix[i], w_main, w_dt2, b_gate[i], conv_w[i], conv_b[i],
                    a_log[i], dt_bias[i], d_skip[i], ssm_norm[i], w_attn_br, w_ssm_br, w_out, norm_mlp[i],
                    w_mlp_in, w_mlp_out, final_gain, i == depth - 1)
    return x2.reshape(bsz, seq, d)
```

```python
import functools
import math

import numpy as np
import jax
import jax.numpy as jnp
from jax import lax
from jax.experimental import pallas as pl
from jax.experimental.pallas import tpu as pltpu

F32 = jnp.float32
BF16 = jnp.bfloat16

D_MODEL = 2048
ATTN_GROUPS = ((128, 1), (512, 4), (2048, 16))
ATTN_HEADS = 8
ATTN_HEAD_DIM = 128
ATTN_WIDTH = ATTN_HEADS * ATTN_HEAD_DIM
ATTN_HALF = 64
ALIBI_SLOPES = tuple(2.0 ** (-8.0 * (j + 1) / ATTN_HEADS) for j in range(ATTN_HEADS))
SSM_INNER = D_MODEL
SSM_HEAD_DIM = 64
SSM_HEADS = SSM_INNER // SSM_HEAD_DIM
SSM_GROUPS = 8
SSM_HEADS_PER_GROUP = SSM_HEADS // SSM_GROUPS
SSM_STATE = 128
SSM_CONV = 5
XBC_WIDTH = SSM_INNER + 2 * SSM_GROUPS * SSM_STATE
GROUP_WIDTH = SSM_HEADS_PER_GROUP * SSM_HEAD_DIM
MLP_HIDDEN = 4 * D_MODEL
NORM_EPS = 1e-6
QKV_GROUP_COLS = 3 * ATTN_WIDTH
QKV_COLS = len(ATTN_GROUPS) * QKV_GROUP_COLS
OFF_Z = QKV_COLS
OFF_XBC = OFF_Z + SSM_INNER
OFF_DT = OFF_XBC + XBC_WIDTH
OFF_GATE = OFF_DT + 2 * SSM_HEADS
IN_WIDTH = OFF_GATE + 2 * D_MODEL

LANES = 128
SSD_CHUNK = 128
CONV_HALO = 16
NEG_BIG = -1e30
LOG2E = math.log2(math.e)
LN2 = math.log(2.0)
VMEM_LIMIT = 56 * 1024 * 1024

NAT_XBC = 0
NAT_GATE = NAT_XBC + XBC_WIDTH
NAT_Z = NAT_GATE + 2 * D_MODEL
NAT_QKV = NAT_Z + SSM_INNER
NAT_WIDTH = NAT_QKV + QKV_GROUP_COLS


def _params(sem, vmem=VMEM_LIMIT):
    return pltpu.CompilerParams(dimension_semantics=sem, vmem_limit_bytes=vmem)


def _dot(a, b):
    return jnp.dot(a, b, preferred_element_type=F32)


def _dot_nt(a, b):
    return lax.dot_general(a, b, (((1,), (1,)), ((), ())), preferred_element_type=F32)


def _dot_tn(a, b):
    return lax.dot_general(a, b, (((0,), (0,)), ((), ())), preferred_element_type=F32)


def _dot_exact(a, b):
    return jnp.dot(a, b, preferred_element_type=F32, precision=lax.Precision.HIGHEST)


def _resident(shape):
    return pl.BlockSpec(shape, lambda *_: (0,) * len(shape), pipeline_mode=pl.Buffered(1))


def _resident_layer(shape, li):
    return pl.BlockSpec((None,) + tuple(shape), lambda *_: (li,) + (0,) * len(shape),
                        pipeline_mode=pl.Buffered(1))


def _mod_kernel(ct_ref, w_ref, b_ref, o_ref):
    bsz = o_ref.shape[0]
    w = w_ref[...]
    rows = [jnp.sum(w * ct_ref[:, b:b + 1], axis=0, keepdims=True) for b in range(bsz)]
    o_ref[...] = jnp.concatenate(rows, axis=0) + b_ref[...]


def _modulation(c, w_mod, b_mod):
    depth, d, n = w_mod.shape
    bsz = c.shape[0]
    tn = 2048
    return pl.pallas_call(
        _mod_kernel,
        out_shape=jax.ShapeDtypeStruct((depth, bsz, n), F32),
        grid=(depth, n // tn),
        in_specs=[
            pl.BlockSpec((d, bsz), lambda l, j: (0, 0)),
            pl.BlockSpec((None, d, tn), lambda l, j: (l, 0, j)),
            pl.BlockSpec((None, 1, tn), lambda l, j: (l, 0, j)),
        ],
        out_specs=pl.BlockSpec((None, bsz, tn), lambda l, j: (l, 0, j)),
        compiler_params=_params(("parallel", "parallel")),
        name="adaln_mod",
    )(c.T, w_mod, b_mod.reshape(depth, 1, n))


_PACK_TR = 512
_PACK_XBC = XBC_WIDTH // _PACK_TR
_PACK_GATE = _PACK_XBC + 2 * D_MODEL // _PACK_TR
_PACK_Z = _PACK_GATE + SSM_INNER // _PACK_TR
_PACK_SHIFT = OFF_GATE % _PACK_TR


def _pack_src_tile(j):
    return jnp.where(j < _PACK_XBC, OFF_XBC // _PACK_TR + j,
                     jnp.where(j < _PACK_GATE, OFF_GATE // _PACK_TR + j - _PACK_XBC,
                               jnp.where(j < _PACK_Z, OFF_Z // _PACK_TR + j - _PACK_GATE, j - _PACK_Z)))


def _pack_kernel(a_ref, b_ref, o_ref, dt_ref):
    j = pl.program_id(1)
    is_gate = (j >= _PACK_XBC) & (j < _PACK_GATE)
    sh = _PACK_SHIFT

    @pl.when(j == _PACK_XBC)
    def _():
        dt_ref[0:sh, :] = a_ref[0:sh, :]
        dt_ref[sh:, :] = jnp.zeros((LANES - sh, a_ref.shape[1]), F32)

    @pl.when(jnp.logical_not(is_gate))
    def _():
        o_ref[...] = a_ref[...].astype(BF16)

    @pl.when(is_gate)
    def _():
        o_ref[0:_PACK_TR - sh, :] = a_ref[sh:, :].astype(BF16)
        o_ref[_PACK_TR - sh:, :] = b_ref[0:sh, :].astype(BF16)


def _pack_in_weights(w_t):
    depth, _, d = w_t.shape
    tr = _PACK_TR
    ntiles = _IN_TILES * _IN_TN // tr
    b_idle = OFF_GATE // tr + 1
    return pl.pallas_call(
        _pack_kernel,
        out_shape=[jax.ShapeDtypeStruct((depth, ntiles * tr, d), BF16),
                   jax.ShapeDtypeStruct((depth, LANES, d), F32)],
        grid=(depth, ntiles),
        in_specs=[
            pl.BlockSpec((None, tr, d), lambda l, j: (l, _pack_src_tile(j), 0)),
            pl.BlockSpec((None, tr, d), lambda l, j: (
                l, jnp.where((j >= _PACK_XBC) & (j < _PACK_GATE), _pack_src_tile(j) + 1, b_idle), 0)),
        ],
        out_specs=[pl.BlockSpec((None, tr, d), lambda l, j: (l, j, 0)),
                   pl.BlockSpec((None, LANES, d), lambda l, j: (l, 0, 0))],
        compiler_params=_params(("parallel", "arbitrary")),
        name="pack_in_weights",
    )(w_t, w_t)


_IN_TM = 1024
_IN_TN = 1024
_NAT_TILES = NAT_WIDTH // _IN_TN
_QKV_TILES = QKV_GROUP_COLS // _IN_TN
_IN_TILES = _NAT_TILES + 2 * _QKV_TILES
_NORM_ROWS = 256
_DEINTERLEAVE_STRIDE = 4


def _modulated_norm(x_ref, gain_ref, shift, scale, r0, rows):
    xf = x_ref[pl.ds(r0, rows), :]
    ms = jnp.mean(xf * xf, axis=-1, keepdims=True)
    y = xf * lax.rsqrt(ms + NORM_EPS) * gain_ref[...]
    return y * (1.0 + scale) + shift


def _inproj_kernel(x_ref, mod_ref, gain_ref, w_ref, wdt_ref, nat_ref, q1_ref, q2_ref, dt_ref, h_scr, r_scr,
                   r2_scr):
    j = pl.program_id(1)
    tm = x_ref.shape[0]

    @pl.when(j == 0)
    def _():
        shift = mod_ref[0:1, :]
        scale = mod_ref[1:2, :]
        for r in range(tm // _NORM_ROWS):
            r0 = r * _NORM_ROWS
            h = _modulated_norm(x_ref, gain_ref, shift, scale, r0, _NORM_ROWS)
            h_hi = h.astype(BF16)
            h_lo = (h - h_hi.astype(F32)).astype(BF16)
            h_scr[pl.ds(r0, _NORM_ROWS), :] = h_hi
            both = _dot_nt(h_hi, wdt_ref[...])
            dt_ref[pl.ds(r0, _NORM_ROWS), :] = (both[:, :LANES] + both[:, LANES:]
                                                + _dot_nt(h_lo, wdt_ref[:LANES, :]))

    @pl.when(j < _NAT_TILES)
    def _():
        nat_ref[...] = _dot_nt(h_scr[...], w_ref[...]).astype(BF16)

    for lo, ref in ((_NAT_TILES, q1_ref), (_NAT_TILES + _QKV_TILES, q2_ref)):
        @pl.when((j >= lo) & (j < lo + _QKV_TILES))
        def _(ref=ref):
            dil, rows = ref.shape[0], ref.shape[1]
            res = _dot_nt(h_scr[...], w_ref[...])
            for cblk in range(_IN_TN // LANES):
                r_scr[cblk] = res[:, cblk * LANES:(cblk + 1) * LANES]
            if dil <= _DEINTERLEAVE_STRIDE:
                for r in range(dil):
                    for cblk in range(_IN_TN // LANES):
                        ref[r, :, cblk * LANES:(cblk + 1) * LANES] = (
                            r_scr[cblk, pl.ds(r, rows, stride=dil), :].astype(BF16))
            else:
                st = _DEINTERLEAVE_STRIDE
                outer = dil // st
                mid = tm // st
                for cblk in range(_IN_TN // LANES):
                    for b in range(st):
                        r2_scr[cblk * st + b] = r_scr[cblk, pl.ds(b, mid, stride=st), :]
                for a in range(outer):
                    for b in range(st):
                        for cblk in range(_IN_TN // LANES):
                            ref[a * st + b, :, cblk * LANES:(cblk + 1) * LANES] = (
                                r2_scr[cblk * st + b, pl.ds(a, rows, stride=outer), :].astype(BF16))


def _in_projection(x2, mod_l, gain, w_main, w_dt, li, bsz):
    t, d = x2.shape
    seq = t // bsz
    tm = min(_IN_TM, seq)
    tpb = seq // tm
    tn = _IN_TN
    d1, d2 = ATTN_GROUPS[1][1], ATTN_GROUPS[2][1]

    def dil_spec(dil, lo):
        return pl.BlockSpec((None, dil, tm // dil, tn),
                            lambda i, j: (i // tpb, 0, i % tpb, jnp.clip(j - lo, 0, _QKV_TILES - 1)))

    return pl.pallas_call(
        _inproj_kernel,
        out_shape=[jax.ShapeDtypeStruct((t, NAT_WIDTH), BF16),
                   jax.ShapeDtypeStruct((bsz, d1, seq // d1, QKV_GROUP_COLS), BF16),
                   jax.ShapeDtypeStruct((bsz, d2, seq // d2, QKV_GROUP_COLS), BF16),
                   jax.ShapeDtypeStruct((t, LANES), F32)],
        grid=(t // tm, _IN_TILES),
        in_specs=[
            pl.BlockSpec((tm, d), lambda i, j: (i, 0)),
            pl.BlockSpec((None, 6, d), lambda i, j: (i // tpb, 0, 0)),
            _resident((1, d)),
            pl.BlockSpec((None, tn, d), lambda i, j: (li, j, 0)),
            _resident_layer((2 * LANES, d), li),
        ],
        out_specs=[pl.BlockSpec((tm, tn), lambda i, j: (i, jnp.minimum(j, _NAT_TILES - 1))),
                   dil_spec(d1, _NAT_TILES),
                   dil_spec(d2, _NAT_TILES + _QKV_TILES),
                   pl.BlockSpec((tm, LANES), lambda i, j: (i, 0))],
        scratch_shapes=[pltpu.VMEM((tm, d), BF16), pltpu.VMEM((tn // LANES, tm, LANES), F32),
                        pltpu.VMEM((tn // LANES * _DEINTERLEAVE_STRIDE, tm // _DEINTERLEAVE_STRIDE, LANES), F32)],
        compiler_params=_params(("parallel", "arbitrary")),
        name="in_projection",
    )(x2, mod_l, gain, w_main, w_dt)


_ATTN_QB = 128
_ATTN_NSUB = 4


def _alibi_table(dil, qb):
    half = ATTN_HALF
    row = np.arange(qb)[:, None]
    col = np.arange(qb)[None, :]
    dist_c = np.abs(col - row)
    hcol = np.arange(2 * half)[None, :]
    dist_p = row + half - hcol
    dist_n = (hcol - half) + qb - row
    dist_h = np.where(hcol < half, dist_p, dist_n)
    dist = np.concatenate([dist_c, dist_h], axis=1).astype(np.float64)
    ok = dist <= half
    slopes = np.asarray(ALIBI_SLOPES, np.float64)[:, None, None]
    tbl = np.where(ok[None], -slopes * dist[None] * dil * LOG2E, NEG_BIG)
    return jnp.asarray(tbl, F32)


def _attn_kernel(q_ref, kc_ref, vc_ref, kp_ref, vp_ref, kn_ref, vn_ref, bias_ref, o_ref, lse_ref,
                 kh_scr, vh_scr, *, nblk):
    j = pl.program_id(2)
    qb = _ATTN_QB
    nsub = q_ref.shape[0] // qb
    half = ATTN_HALF
    c1 = ATTN_HEAD_DIM ** -0.5 * LOG2E
    hcol = lax.broadcasted_iota(jnp.int32, (qb, 2 * half), 1)
    lane = lax.broadcasted_iota(jnp.int32, (qb, LANES), 1)
    ones = jnp.ones((qb, LANES), BF16)
    for s in range(nsub):
        rows = pl.ds(s * qb, qb)
        edge_ok = None
        if s == 0:
            kh_scr[s, 0:half, :] = kp_ref[...]
            vh_scr[s, 0:half, :] = vp_ref[...]
            edge_ok = hcol >= jnp.where(j > 0, 0, half)
        else:
            kh_scr[s, 0:half, :] = kc_ref[pl.ds(s * qb - half, half), :]
            vh_scr[s, 0:half, :] = vc_ref[pl.ds(s * qb - half, half), :]
        if s == nsub - 1:
            kh_scr[s, half:2 * half, :] = kn_ref[...]
            vh_scr[s, half:2 * half, :] = vn_ref[...]
            ok_n = hcol < jnp.where(j < nblk - 1, 2 * half, half)
            edge_ok = ok_n if edge_ok is None else edge_ok & ok_n
        else:
            kh_scr[s, half:2 * half, :] = kc_ref[pl.ds((s + 1) * qb, half), :]
            vh_scr[s, half:2 * half, :] = vc_ref[pl.ds((s + 1) * qb, half), :]
        lse_all = jnp.zeros((qb, LANES), F32)
        for h in range(ATTN_HEADS):
            sl = slice(h * ATTN_HEAD_DIM, (h + 1) * ATTN_HEAD_DIM)
            q = q_ref[rows, sl]
            t_c = _dot_nt(q, kc_ref[rows, sl]) * c1 + bias_ref[h, :, 0:qb]
            t_h = _dot_nt(q, kh_scr[s, :, sl]) * c1 + bias_ref[h, :, qb:]
            if edge_ok is not None:
                t_h = jnp.where(edge_ok, t_h, NEG_BIG)
            m = jnp.max(jnp.maximum(t_c, t_h), axis=-1, keepdims=True)
            p_c = jnp.exp2(t_c - m).astype(BF16)
            p_h = jnp.exp2(t_h - m).astype(BF16)
            acc = (_dot(p_c, jnp.concatenate([vc_ref[rows, sl], ones], axis=1))
                   + _dot(p_h, jnp.concatenate([vh_scr[s, :, sl], ones], axis=1)))
            l = acc[:, LANES:LANES + 1]
            o_ref[rows, sl] = (acc[:, :LANES] * (1.0 / l)).astype(BF16)
            lse_all = jnp.where(lane == h, (m + jnp.log2(l)) * LN2, lse_all)
        lse_ref[rows, :] = lse_all


def _attention_group(qkv, part0, dil):
    bsz, _, n, _ = qkv.shape
    nsub = min(_ATTN_NSUB, n // _ATTN_QB)
    qb = nsub * _ATTN_QB
    nblk = n // qb
    half = ATTN_HALF
    hb = qb // half
    nhalf = n // half
    w = ATTN_WIDTH
    cur = lambda part: pl.BlockSpec((None, None, qb, w), lambda b, r, j: (b, r, j, part0 + part))
    prev = lambda part: pl.BlockSpec(
        (None, None, half, w), lambda b, r, j: (b, r, jnp.maximum(j * hb - 1, 0), part0 + part))
    nxt = lambda part: pl.BlockSpec(
        (None, None, half, w), lambda b, r, j: (b, r, jnp.minimum((j + 1) * hb, nhalf - 1), part0 + part))
    return pl.pallas_call(
        functools.partial(_attn_kernel, nblk=nblk),
        out_shape=[jax.ShapeDtypeStruct((bsz, dil, n, w), BF16),
                   jax.ShapeDtypeStruct((bsz, dil, n, LANES), F32)],
        grid=(bsz, dil, nblk),
        in_specs=[cur(0), cur(1), cur(2), prev(1), prev(2), nxt(1), nxt(2),
                  _resident((ATTN_HEADS, _ATTN_QB, 2 * _ATTN_QB))],
        out_specs=[pl.BlockSpec((None, None, qb, w), lambda b, r, j: (b, r, j, 0)),
                   pl.BlockSpec((None, None, qb, LANES), lambda b, r, j: (b, r, j, 0))],
        scratch_shapes=[pltpu.VMEM((nsub, 2 * half, w), BF16), pltpu.VMEM((nsub, 2 * half, w), BF16)],
        compiler_params=_params(("parallel", "parallel", "parallel")),
        name=f"band_attention_d{dil}",
    )(qkv, qkv, qkv, qkv, qkv, qkv, qkv, _alibi_table(dil, _ATTN_QB))


def _softplus(x):
    return jnp.maximum(x, 0.0) + jnp.log(1.0 + jnp.exp(-jnp.abs(x)))


def _head_expansion_matrix(first_lane):
    e = np.zeros((LANES, SSM_INNER), np.float32)
    for h in range(SSM_HEADS):
        e[first_lane + h, h * SSM_HEAD_DIM:(h + 1) * SSM_HEAD_DIM] = 1.0
    return jnp.asarray(np.concatenate([e, e, e], axis=0), BF16)


def _expand_heads(q, e3_ref):
    return _dot(jnp.concatenate(_split3(q), axis=1), e3_ref[...])


def _dt_terms(dt_ref, par_ref):
    a_row = -jnp.exp(par_ref[0:1, :])
    dtv = _softplus(dt_ref[...] + par_ref[1:2, :])
    return dtv, dtv * a_row


def _ssd_fwd_kernel(xm_ref, xp_ref, xn_ref, dt_ref, cw_ref, cb_ref, par_ref, dsk_ref, e3_ref,
                    xc_ref, y1_ref, xf_scr, st_scr, *, nc):
    c = pl.program_id(1)
    L = SSD_CHUNK
    H = SSM_HEADS
    R = SSM_HEADS_PER_GROUP
    N = SSM_STATE
    GW = GROUP_WIDTH

    @pl.when(c == 0)
    def _():
        st_scr[...] = jnp.zeros_like(st_scr)

    has_prev = c > 0
    has_next = c < nc - 1
    cwid = 512
    for cc in range(XBC_WIDTH // cwid):
        cols = slice(cc * cwid, (cc + 1) * cwid)
        xm = xm_ref[:, cols].astype(F32)
        xp = jnp.where(has_prev, xp_ref[:, cols].astype(F32), 0.0)
        xn = jnp.where(has_next, xn_ref[:, cols].astype(F32), 0.0)
        xcat = jnp.concatenate([xp, xm, xn], axis=0)
        rows_cat = L + 2 * CONV_HALO
        acc = cb_ref[:, cols] + cw_ref[SSM_CONV // 2:SSM_CONV // 2 + 1, cols] * xm
        for k in range(SSM_CONV):
            if k != SSM_CONV // 2:
                shifted = pltpu.roll(xcat, (SSM_CONV // 2 - k) % rows_cat, axis=0)[CONV_HALO:CONV_HALO + L]
                acc = acc + cw_ref[k:k + 1, cols] * shifted
        sv = acc * (1.0 / (1.0 + jnp.exp(-acc)))
        xf_scr[:, cols] = sv
        xc_ref[:, cols] = sv.astype(BF16)

    dtv, dta = _dt_terms(dt_ref, par_ref)
    ti = lax.broadcasted_iota(jnp.int32, (L, L), 0)
    si = lax.broadcasted_iota(jnp.int32, (L, L), 1)
    lower = si <= ti
    strict_lower = si < ti
    strict_upper = si > ti
    a_f = _dot_exact(lower.astype(F32), dta)
    g_b = _dot_exact((si >= ti).astype(F32), dta)
    a_f_t = a_f.T
    g_b_t = g_b.T
    dt_t = dtv.T
    exp_af = jnp.exp(a_f)
    a_end = a_f[L - 1:L, :]
    w_end = jnp.exp(a_end - a_f) * dtv
    cd_row = jnp.exp(a_end)
    lane = lax.broadcasted_iota(jnp.int32, (L, GW), 1)
    fac = _expand_heads(jnp.concatenate([exp_af, w_end, jnp.broadcast_to(cd_row, (8, LANES))], axis=0), e3_ref)

    for g in range(SSM_GROUPS):
        gsl = slice(g * GW, (g + 1) * GW)
        xg = xf_scr[:, gsl]
        xg16 = xg.astype(BF16)
        bg = xf_scr[:, SSM_INNER + g * N:SSM_INNER + (g + 1) * N]
        cg16 = xf_scr[:, SSM_INNER + SSM_GROUPS * N + g * N:
                      SSM_INNER + SSM_GROUPS * N + (g + 1) * N].astype(BF16)
        bg16 = bg.astype(BF16)
        cb = _dot_nt(cg16, bg16)
        ms, xs = [], []
        for r in range(R):
            h = g * R + r
            e = jnp.where(lower, a_f[:, h:h + 1] - a_f_t[h:h + 1, :],
                          g_b[:, H + h:H + h + 1] - g_b_t[H + h:H + h + 1, :])
            dtf_row = dt_t[h:h + 1, :]
            dtb_row = dt_t[H + h:H + h + 1, :]
            dsel = jnp.where(strict_lower, dtf_row, jnp.where(strict_upper, dtb_row, dtf_row + dtb_row))
            ms.append((cb * (jnp.exp(e) * dsel)).astype(BF16))
            xs.append(jnp.where((lane >= r * SSM_HEAD_DIM) & (lane < (r + 1) * SSM_HEAD_DIM),
                                xg16, jnp.zeros_like(xg16)))
        y = _dot(jnp.concatenate(ms, axis=1), jnp.concatenate(xs, axis=0))
        st = st_scr[g]
        y = y + _dot(cg16, st.astype(BF16)) * fac[0:L, gsl]
        y = y + xg * dsk_ref[:, gsl]
        y1_ref[:, gsl] = y.astype(BF16)
        xw = (xg * fac[L:2 * L, gsl]).astype(BF16)
        s_new = _dot_tn(bg16, xw)
        st_scr[g] = st * fac[2 * L:2 * L + 1, gsl] + s_new


def _ssd_bwd_kernel(xc_ref, dt_ref, par_ref, y1_ref, z_ref, ng_ref, e3_ref, y_ref, st_scr):
    c = pl.program_id(1)
    L = SSD_CHUNK
    H = SSM_HEADS
    R = SSM_HEADS_PER_GROUP
    N = SSM_STATE
    GW = GROUP_WIDTH

    @pl.when(c == 0)
    def _():
        st_scr[...] = jnp.zeros_like(st_scr)

    dtv, dta = _dt_terms(dt_ref, par_ref)
    ti = lax.broadcasted_iota(jnp.int32, (L, L), 0)
    si = lax.broadcasted_iota(jnp.int32, (L, L), 1)
    g_b = _dot_exact((si >= ti).astype(F32), dta)
    exp_gb = jnp.exp(g_b)
    g_start = g_b[0:1, :]
    w_start = jnp.exp(g_start - g_b) * dtv
    cd_row = jnp.exp(g_start)
    fac = _expand_heads(jnp.concatenate([exp_gb, w_start, jnp.broadcast_to(cd_row, (8, LANES))], axis=0), e3_ref)

    for g in range(SSM_GROUPS):
        gsl = slice(g * GW, (g + 1) * GW)
        xg = xc_ref[:, gsl].astype(F32)
        bg16 = xc_ref[:, SSM_INNER + g * N:SSM_INNER + (g + 1) * N]
        cg16 = xc_ref[:, SSM_INNER + SSM_GROUPS * N + g * N:SSM_INNER + SSM_GROUPS * N + (g + 1) * N]
        st = st_scr[g]
        y = y1_ref[:, gsl].astype(F32) + _dot(cg16, st.astype(BF16)) * fac[0:L, gsl]
        zg = z_ref[:, gsl].astype(F32)
        y = y * (zg * (1.0 / (1.0 + jnp.exp(-zg))))
        ms = jnp.mean(y * y, axis=-1, keepdims=True)
        y_ref[:, gsl] = (y * lax.rsqrt(ms + NORM_EPS) * ng_ref[:, gsl]).astype(BF16)
        xw = (xg * fac[L:2 * L, gsl]).astype(BF16)
        s_new = _dot_tn(bg16, xw)
        st_scr[g] = st * fac[2 * L:2 * L + 1, gsl] + s_new


def _bidirectional_ssd(nat, dt, conv_w, conv_b, ssm_par, d_skip_x, norm_g, bsz, seq):
    L = SSD_CHUNK
    nc = seq // L
    hpc = L // CONV_HALO
    nhalo = seq // CONV_HALO
    nat3 = nat.reshape(bsz, seq, NAT_WIDTH)
    dt3 = dt.reshape(bsz, seq, LANES)
    xbc_blk = NAT_XBC // XBC_WIDTH
    z_blk = NAT_Z // SSM_INNER
    state = pltpu.VMEM((SSM_GROUPS, SSM_STATE, GROUP_WIDTH), F32)
    xc, y1 = pl.pallas_call(
        functools.partial(_ssd_fwd_kernel, nc=nc),
        out_shape=[jax.ShapeDtypeStruct((bsz, seq, XBC_WIDTH), BF16),
                   jax.ShapeDtypeStruct((bsz, seq, SSM_INNER), BF16)],
        grid=(bsz, nc),
        in_specs=[
            pl.BlockSpec((None, L, XBC_WIDTH), lambda b, c: (b, c, xbc_blk)),
            pl.BlockSpec((None, CONV_HALO, XBC_WIDTH), lambda b, c: (b, jnp.maximum(c * hpc - 1, 0), xbc_blk)),
            pl.BlockSpec((None, CONV_HALO, XBC_WIDTH),
                         lambda b, c: (b, jnp.minimum((c + 1) * hpc, nhalo - 1), xbc_blk)),
            pl.BlockSpec((None, L, LANES), lambda b, c: (b, c, 0)),
            _resident((SSM_CONV, XBC_WIDTH)), _resident((1, XBC_WIDTH)), _resident((8, LANES)),
            _resident((1, SSM_INNER)), _resident((3 * LANES, SSM_INNER)),
        ],
        out_specs=[pl.BlockSpec((None, L, XBC_WIDTH), lambda b, c: (b, c, 0)),
                   pl.BlockSpec((None, L, SSM_INNER), lambda b, c: (b, c, 0))],
        scratch_shapes=[pltpu.VMEM((L, XBC_WIDTH), F32), state],
        compiler_params=_params(("parallel", "arbitrary")),
        name="ssd_forward_sweep",
    )(nat3, nat3, nat3, dt3, conv_w, conv_b, ssm_par, d_skip_x, _head_expansion_matrix(0))
    rev = lambda width, blk=0: pl.BlockSpec((None, L, width), lambda b, c: (b, nc - 1 - c, blk))
    y = pl.pallas_call(
        _ssd_bwd_kernel,
        out_shape=jax.ShapeDtypeStruct((bsz, seq, SSM_INNER), BF16),
        grid=(bsz, nc),
        in_specs=[rev(XBC_WIDTH), rev(LANES), _resident((8, LANES)), rev(SSM_INNER),
                  rev(SSM_INNER, z_blk), _resident((1, SSM_INNER)), _resident((3 * LANES, SSM_INNER))],
        out_specs=rev(SSM_INNER),
        scratch_shapes=[state],
        compiler_params=_params(("parallel", "arbitrary")),
        name="ssd_backward_sweep",
    )(xc, dt3, ssm_par, y1, nat3, norm_g, _head_expansion_matrix(SSM_HEADS))
    return y.reshape(bsz * seq, SSM_INNER)


_MERGE_TM = 512
_MERGE_NB = 512
_MERGE_PB = 256


def _interleave_matrix(tm, dil):
    p = np.zeros((tm, tm), np.float32)
    t = np.arange(tm)
    p[t, (t % dil) * (tm // dil) + t // dil] = 1.0
    return jnp.asarray(p, BF16)


def _split3(x):
    hi = x.astype(BF16)
    r1 = x - hi.astype(F32)
    mid = r1.astype(BF16)
    lo = (r1 - mid.astype(F32)).astype(BF16)
    return hi, mid, lo


def _merge_kernel(o0_ref, o1_ref, o2_ref, l0_ref, l1_ref, l2_ref, p1_ref, p2_ref, ys_ref, gate_ref, bg_ref,
                  wa_ref, ws_ref, out_ref, ya_scr, s_scr):
    tm, d = out_ref.shape

    def token_order(o_ref, l_ref, p_ref):
        dil = o_ref.shape[0]
        pb = p_ref.shape[0]
        rpb = pb // dil
        w = o_ref.shape[2]
        o_tok, l_tok = [], []
        for blk in range(tm // pb):
            rs = slice(blk * rpb, (blk + 1) * rpb)
            flat = jnp.concatenate(
                [jnp.concatenate((o_ref[r, rs, :],) + _split3(l_ref[r, rs, :]), axis=1) for r in range(dil)],
                axis=0)
            res = _dot(p_ref[...], flat)
            o_tok.append(res[:, :w])
            l_tok.append(res[:, w:w + LANES] + res[:, w + LANES:w + 2 * LANES] + res[:, w + 2 * LANES:])
        return jnp.concatenate(o_tok, axis=0), jnp.concatenate(l_tok, axis=0)

    o1, l1 = token_order(o1_ref, l1_ref, p1_ref)
    o2, l2 = token_order(o2_ref, l2_ref, p2_ref)
    nb = _MERGE_NB
    for cb in range(d // nb):
        cs = slice(cb * nb, (cb + 1) * nb)
        s_scr[:, cs] = _dot(ys_ref[...], ws_ref[:, cs])
    l0 = l0_ref[...]
    m = jnp.maximum(l0, jnp.maximum(l1, l2))
    e0, e1, e2 = jnp.exp(l0 - m), jnp.exp(l1 - m), jnp.exp(l2 - m)
    inv = 1.0 / (e0 + e1 + e2)
    w0, w1, w2 = e0 * inv, e1 * inv, e2 * inv
    for h in range(ATTN_HEADS):
        sl = slice(h * ATTN_HEAD_DIM, (h + 1) * ATTN_HEAD_DIM)
        ya_scr[:, sl] = (w0[:, h:h + 1] * o0_ref[:, sl].astype(F32) + w1[:, h:h + 1] * o1[:, sl]
                         + w2[:, h:h + 1] * o2[:, sl]).astype(BF16)
    for cb in range(d // nb):
        cs = slice(cb * nb, (cb + 1) * nb)
        cs2 = slice(d + cb * nb, d + (cb + 1) * nb)
        ga = gate_ref[:, cs].astype(F32) + bg_ref[:, cs]
        gs = gate_ref[:, cs2].astype(F32) + bg_ref[:, cs2]
        a = _dot(ya_scr[...], wa_ref[:, cs])
        s = s_scr[:, cs]
        out_ref[:, cs] = (a * (1.0 / (1.0 + jnp.exp(-ga))) + s * (1.0 / (1.0 + jnp.exp(-gs)))).astype(BF16)


def _branch_merge(o0, o1, o2, l0, l1, l2, y_ssm, nat, b_gate, w_attn_br, w_ssm_br, li, bsz):
    t = y_ssm.shape[0]
    d = D_MODEL
    seq = t // bsz
    tm = min(_MERGE_TM, seq // ATTN_GROUPS[2][1])
    tpb = seq // tm
    pb = min(_MERGE_PB, tm)
    row = lambda width, blk=0: pl.BlockSpec((tm, width), lambda i: (i, blk))

    def dil_spec(arr):
        dil, width = arr.shape[1], arr.shape[3]
        return pl.BlockSpec((None, dil, tm // dil, width), lambda i: (i // tpb, 0, i % tpb, 0))

    return pl.pallas_call(
        _merge_kernel,
        out_shape=jax.ShapeDtypeStruct((t, d), BF16),
        grid=(t // tm,),
        in_specs=[row(ATTN_WIDTH), dil_spec(o1), dil_spec(o2), row(LANES), dil_spec(l1), dil_spec(l2),
                  _resident((pb, pb)), _resident((pb, pb)),
                  row(SSM_INNER), row(2 * d, NAT_GATE // (2 * d)), _resident((1, 2 * d)),
                  _resident_layer((ATTN_WIDTH, d), li), _resident_layer((SSM_INNER, d), li)],
        out_specs=row(d),
        scratch_shapes=[pltpu.VMEM((tm, ATTN_WIDTH), BF16), pltpu.VMEM((tm, d), F32)],
        compiler_params=_params(("parallel",)),
        name="branch_merge",
    )(o0, o1, o2, l0, l1, l2, _interleave_matrix(pb, o1.shape[1]), _interleave_matrix(pb, o2.shape[1]),
      y_ssm, nat, b_gate, w_attn_br, w_ssm_br)


def _outproj_kernel(mg_ref, w_ref, x_ref, mod_ref, gain_ref, xo_ref, h2_ref):
    xn = x_ref[...] + mod_ref[2:3, :] * _dot(mg_ref[...], w_ref[...])
    xo_ref[...] = xn
    ms = jnp.mean(xn * xn, axis=-1, keepdims=True)
    y = xn * lax.rsqrt(ms + NORM_EPS) * gain_ref[...]
    h2_ref[...] = (y * (1.0 + mod_ref[4:5, :]) + mod_ref[3:4, :]).astype(BF16)


def _out_projection(merged, w_out, x2, mod_l, gain, li, bsz):
    t, d = x2.shape
    seq = t // bsz
    tm = min(512, seq)
    tpb = seq // tm
    row = lambda: pl.BlockSpec((tm, d), lambda i: (i, 0))
    return pl.pallas_call(
        _outproj_kernel,
        out_shape=[jax.ShapeDtypeStruct((t, d), F32), jax.ShapeDtypeStruct((t, d), BF16)],
        grid=(t // tm,),
        in_specs=[row(), _resident_layer((d, d), li), row(),
                  pl.BlockSpec((None, 6, d), lambda i: (i // tpb, 0, 0)),
                  _resident((1, d))],
        out_specs=[row(), row()],
        compiler_params=_params(("parallel",)),
        name="out_projection",
    )(merged, w_out, x2, mod_l, gain)


_MLP_TM = 1024
_MLP_TK = 512
_MLP_NB = 512
_MLP_ROWS = 256


def _mlp_kernel(h_ref, w1_ref, w2_ref, x_ref, mod_ref, fg_ref, o_ref, *, nk, final_norm):
    k = pl.program_id(1)
    tm, d = o_ref.shape

    @pl.when(k == 0)
    def _():
        o_ref[...] = jnp.zeros_like(o_ref)

    u = jnp.maximum(_dot(h_ref[...], w1_ref[...]), 0.0)
    u2 = (u * u).astype(BF16)
    for cb in range(d // _MLP_NB):
        cs = slice(cb * _MLP_NB, (cb + 1) * _MLP_NB)
        o_ref[:, cs] += _dot(u2, w2_ref[:, cs])

    @pl.when(k == nk - 1)
    def _():
        for r in range(tm // _MLP_ROWS):
            rs = pl.ds(r * _MLP_ROWS, _MLP_ROWS)
            xn = x_ref[rs, :] + mod_ref[5:6, :] * o_ref[rs, :]
            if final_norm:
                ms = jnp.mean(xn * xn, axis=-1, keepdims=True)
                xn = xn * lax.rsqrt(ms + NORM_EPS) * fg_ref[...]
            o_ref[rs, :] = xn


def _mlp(h2, w1, w2, x2, mod_l, final_gain, li, bsz, final_norm):
    t, d = x2.shape
    seq = t // bsz
    tm = min(_MLP_TM, seq)
    tk = _MLP_TK
    nk = w1.shape[2] // tk
    tpb = seq // tm
    return pl.pallas_call(
        functools.partial(_mlp_kernel, nk=nk, final_norm=final_norm),
        out_shape=jax.ShapeDtypeStruct((t, d), F32),
        grid=(t // tm, nk),
        in_specs=[pl.BlockSpec((tm, d), lambda i, k: (i, 0)),
                  pl.BlockSpec((None, d, tk), lambda i, k: (li, 0, k)),
                  pl.BlockSpec((None, tk, d), lambda i, k: (li, k, 0)),
                  pl.BlockSpec((tm, d), lambda i, k: (i, 0), pipeline_mode=pl.Buffered(1)),
                  pl.BlockSpec((None, 6, d), lambda i, k: (i // tpb, 0, 0)),
                  _resident((1, d))],
        out_specs=pl.BlockSpec((tm, d), lambda i, k: (i, 0)),
        compiler_params=_params(("parallel", "arbitrary")),
        name="relu2_mlp",
    )(h2, w1, w2, x2, mod_l, final_gain)


def _layer(x2, bsz, seq, li, mod_l, norm_mix, w_main, w_dt2, b_gate, conv_w, conv_b, a_log, dt_bias, d_skip,
           ssm_norm, w_attn_br, w_ssm_br, w_out, norm_mlp, w_mlp_in, w_mlp_out, final_gain, final_norm):
    d = D_MODEL
    ssm_par = jnp.pad(jnp.stack([a_log.reshape(-1), dt_bias.reshape(-1)]),
                      ((0, 6), (0, LANES - 2 * SSM_HEADS)))
    d_skip_x = jnp.repeat(d_skip, SSM_HEAD_DIM).reshape(1, SSM_INNER)

    nat, q1, q2, dt = _in_projection(x2, mod_l, norm_mix.reshape(1, d), w_main, w_dt2, li, bsz)
    nat4 = nat.reshape(bsz, 1, seq, NAT_WIDTH)
    o0, l0 = _attention_group(nat4, NAT_QKV // ATTN_WIDTH, 1)
    o1, l1 = _attention_group(q1, 0, ATTN_GROUPS[1][1])
    o2, l2 = _attention_group(q2, 0, ATTN_GROUPS[2][1])
    y_ssm = _bidirectional_ssd(nat, dt, conv_w.reshape(SSM_CONV, XBC_WIDTH), conv_b.reshape(1, XBC_WIDTH),
                               ssm_par, d_skip_x, ssm_norm.reshape(1, SSM_INNER), bsz, seq)
    merged = _branch_merge(o0.reshape(bsz * seq, ATTN_WIDTH), o1, o2, l0.reshape(bsz * seq, LANES), l1, l2,
                           y_ssm, nat, b_gate.reshape(1, 2 * d), w_attn_br, w_ssm_br, li, bsz)
    x2, h2 = _out_projection(merged, w_out, x2, mod_l, norm_mlp.reshape(1, d), li, bsz)
    return _mlp(h2, w_mlp_in, w_mlp_out, x2, mod_l, final_gain, li, bsz, final_norm)


def kernel(x, c, w_mod, b_mod, norm_mix, w_in, b_gate, conv_w, conv_b, a_log, dt_bias, d_skip, ssm_norm,
           w_attn_br, w_ssm_br, w_out, norm_mlp, w_mlp_in, w_mlp_out, norm_final):
    bsz, seq, d = x.shape
    depth = w_mod.shape[0]
    mod = _modulation(c, w_mod, b_mod).reshape(depth, bsz, 6, d)
    x2 = x.reshape(bsz * seq, d)
    final_gain = norm_final.reshape(1, d)
    w_main, w_dt = _pack_in_weights(jnp.swapaxes(w_in, 1, 2))
    w_dt_hi = w_dt.astype(BF16)
    w_dt_lo = (w_dt - w_dt_hi.astype(F32)).astype(BF16)
    w_dt2 = jnp.concatenate([w_dt_hi, w_dt_lo], axis=1)
    w_attn_br, w_ssm_br, w_out, w_mlp_in, w_mlp_out = (
        w.astype(BF16) for w in (w_attn_br, w_ssm_br, w_out, w_mlp_in, w_mlp_out))
    for i in range(depth):
        x2 = _layer(x2, bsz, seq, i, mod[i], norm_mix[i], w_main, w_dt2, b_gate[i], conv_w[i], conv_b[i],
                    a_log[i], dt_bias[i], d_skip[i], ssm_norm[i], w_attn_br, w_ssm_br, w_out, norm_mlp[i],
                    w_mlp_in, w_mlp_out, final_gain, i == depth - 1)
    return x2.reshape(bsz, seq, d)
```
